```python
import jax, jax.numpy as jnp
from jax import lax
import numpy as np

D_MODEL = 1024
BATCH = 8
SEQ = 2048
DEPTH = 4
DEC_BATCH = 32
DEC_SEQ = 4
PAST_LEN = 8192
PAGE_SIZE = 128

N_META = 16
N_HEADS = 8
HEAD_DIM = 64
ATT_W = N_HEADS * HEAD_DIM
POOL_WINDOWS = (2, 4, 8, 16)
N_POOL_GROUPS = len(POOL_WINDOWS)
POOL_GROUP_W = D_MODEL // 8
POOL_W = N_POOL_GROUPS * POOL_GROUP_W
POOL_HIST = max(POOL_WINDOWS) - 1
D_FF = 2816
Q_BLOCK = 128
RMS_EPS = 1e-6
IN_W = POOL_W + 3 * ATT_W + N_HEADS + 2 * D_MODEL
FORGET_BIAS_LO = 1.0
FORGET_BIAS_HI = 6.0

kernel_name = "hybrid_pool_fox_macaron_step"


def rmsnorm(x, g):
    xf = x.astype(jnp.float32)
    y = xf * lax.rsqrt(jnp.mean(xf * xf, axis=-1, keepdims=True) + RMS_EPS)
    return (y * g.astype(jnp.float32)).astype(x.dtype)


def swiglu(h, wi, wo):
    gate, up = jnp.split(h @ wi, 2, axis=-1)
    return (jax.nn.silu(gate) * up) @ wo


def ffn_half_step(x, g_pre, g_post, wi, wo):
    return x + 0.5 * rmsnorm(swiglu(rmsnorm(x, g_pre), wi, wo), g_post)


def project_in(h, w_in, b_f):
    B, T = h.shape[:2]
    z = h @ w_in
    o1 = POOL_W
    o2 = o1 + ATT_W
    o3 = o2 + ATT_W
    o4 = o3 + ATT_W
    o5 = o4 + N_HEADS
    o6 = o5 + D_MODEL
    u = z[..., :o1]
    q = z[..., o1:o2].reshape(B, T, N_HEADS, HEAD_DIM)
    k = z[..., o2:o3].reshape(B, T, N_HEADS, HEAD_DIM)
    v = z[..., o3:o4].reshape(B, T, N_HEADS, HEAD_DIM)
    logf = jax.nn.log_sigmoid((z[..., o4:o5] + b_f).astype(jnp.float32))
    gp = z[..., o5:o6]
    ga = z[..., o6:]
    return u, q, k, v, logf, gp, ga


def pool_mix(u_ext, n_prev, pos0, w_grp, pool_scale):
    B = u_ext.shape[0]
    T = u_ext.shape[1] - n_prev
    uf = u_ext.astype(jnp.float32)
    s0 = jnp.concatenate([jnp.zeros_like(uf[:, :1]), jnp.cumsum(uf, axis=1)], axis=1)
    end = n_prev + jnp.arange(T) + 1
    pos = pos0 + jnp.arange(T)
    means = []
    for g, w in enumerate(POOL_WINDOWS):
        sl = slice(g * POOL_GROUP_W, (g + 1) * POOL_GROUP_W)
        start = jnp.maximum(end - w, 0)
        tot = s0[:, end, sl] - s0[:, start, sl]
        cnt = jnp.minimum(w, pos + 1).astype(jnp.float32)
        means.append(tot / cnt[None, :, None])
    d = jnp.concatenate(means, axis=-1) - uf[:, n_prev:]
    d = d.reshape(B, T, N_POOL_GROUPS, POOL_GROUP_W)
    mixed = jnp.einsum('btgc,gcd->btgd', d, w_grp.astype(jnp.float32)).reshape(B, T, POOL_W)
    return (mixed * pool_scale.astype(jnp.float32)).astype(u_ext.dtype)


def fox_attend(q, k, v, cq, ck, pos_q, pos_k):
    s = jnp.einsum('bqhd,bkhd->bhqk', q, k, preferred_element_type=jnp.float32) * (HEAD_DIM ** -0.5)
    bias = jnp.transpose(cq, (0, 2, 1))[:, :, :, None] - jnp.transpose(ck, (0, 2, 1))[:, :, None, :]
    mask = pos_k[None, :] <= pos_q[:, None]
    s = jnp.where(mask, s + bias, jnp.finfo(jnp.float32).min)
    p = jax.nn.softmax(s, axis=-1)
    return jnp.einsum('bhqk,bkhd->bqhd', p.astype(v.dtype), v)


def prompt_attention(q, k, v, c):
    B, L = q.shape[:2]
    pos = jnp.arange(L)
    o_meta = fox_attend(q[:, :N_META], k[:, :N_META], v[:, :N_META],
                        c[:, :N_META], c[:, :N_META], pos[:N_META], pos[:N_META])
    n_blk = (L - N_META) // Q_BLOCK
    qb = q[:, N_META:].reshape(B, n_blk, Q_BLOCK, N_HEADS, HEAD_DIM).transpose(1, 0, 2, 3, 4)
    cb = c[:, N_META:].reshape(B, n_blk, Q_BLOCK, N_HEADS).transpose(1, 0, 2, 3)
    pb = pos[N_META:].reshape(n_blk, Q_BLOCK)
    ob = lax.map(lambda a: fox_attend(a[0], k, v, a[1], c, a[2], pos), (qb, cb, pb))
    o_real = ob.transpose(1, 0, 2, 3, 4).reshape(B, L - N_META, N_HEADS, HEAD_DIM)
    return jnp.concatenate([o_meta, o_real], axis=1)


def merge(pool_u, att_o, gp, ga, w_pool_up, w_att_up, w_out):
    a = att_o.reshape(att_o.shape[0], att_o.shape[1], ATT_W)
    m = jax.nn.sigmoid(gp) * (pool_u @ w_pool_up) + jax.nn.sigmoid(ga) * (a @ w_att_up)
    return m @ w_out


def setup_inputs(seed: int = 0) -> dict:
    key = jax.random.key(seed)
    ks = jax.random.split(key, 32)
    f32 = jnp.float32
    n_pages = PAST_LEN // PAGE_SIZE
    n_used = DEC_BATCH * n_pages
    n_pool = n_used + n_used // 4
    nrm = lambda k, shape, scale: jax.random.normal(k, shape, f32) * scale
    gain = lambda k: 1.0 + 0.05 * jax.random.normal(k, (DEPTH, D_MODEL), f32)
    head_bias = jnp.linspace(FORGET_BIAS_LO, FORGET_BIAS_HI, N_HEADS, dtype=f32)
    perm = jax.random.permutation(ks[0], n_pool)
    page_table = perm[:n_used].reshape(DEC_BATCH, n_pages).astype(jnp.int32)
    return {
        "x_prompt": nrm(ks[1], (BATCH, SEQ, D_MODEL), 1.0),
        "x_sample": nrm(ks[2], (DEC_BATCH, DEC_SEQ, D_MODEL), 1.0),
        "cache_k": nrm(ks[3], (DEPTH, n_pool, PAGE_SIZE, N_HEADS, HEAD_DIM), 1.0),
        "cache_v": nrm(ks[4], (DEPTH, n_pool, PAGE_SIZE, N_HEADS, HEAD_DIM), 1.0),
        "cache_logf": jax.nn.log_sigmoid(nrm(ks[5], (DEPTH, n_pool, PAGE_SIZE, N_HEADS), 1.0) + head_bias),
        "state_pool": nrm(ks[6], (DEPTH, DEC_BATCH, POOL_HIST, POOL_W), 1.0),
        "page_table": page_table,
        "meta_tokens": nrm(ks[7], (N_META, D_MODEL), 1.0),
        "ln_ffn1_pre": gain(ks[8]),
        "ln_ffn1_post": gain(ks[9]),
        "ffn1_wi": nrm(ks[10], (DEPTH, D_MODEL, 2 * D_FF), D_MODEL ** -0.5),
        "ffn1_wo": nrm(ks[11], (DEPTH, D_FF, D_MODEL), D_FF ** -0.5),
        "ln_mix_pre": gain(ks[12]),
        "ln_mix_post": gain(ks[13]),
        "w_in": nrm(ks[14], (DEPTH, D_MODEL, IN_W), D_MODEL ** -0.5),
        "b_forget": head_bias[None, :] + nrm(ks[15], (DEPTH, N_HEADS), 0.1),
        "pool_w_grp": nrm(ks[16], (DEPTH, N_POOL_GROUPS, POOL_GROUP_W, POOL_GROUP_W), POOL_GROUP_W ** -0.5),
        "pool_scale": 1.0 + 0.1 * jax.random.normal(ks[17], (DEPTH, POOL_W), f32),
        "w_pool_up": nrm(ks[18], (DEPTH, POOL_W, D_MODEL), POOL_W ** -0.5),
        "w_att_up": nrm(ks[19], (DEPTH, ATT_W, D_MODEL), ATT_W ** -0.5),
        "w_out": nrm(ks[20], (DEPTH, D_MODEL, D_MODEL), D_MODEL ** -0.5),
        "ln_ffn2_pre": gain(ks[21]),
        "ln_ffn2_post": gain(ks[22]),
        "ffn2_wi": nrm(ks[23], (DEPTH, D_MODEL, 2 * D_FF), D_MODEL ** -0.5),
        "ffn2_wo": nrm(ks[24], (DEPTH, D_FF, D_MODEL), D_FF ** -0.5),
    }


def reference(x_prompt, x_sample, cache_k, cache_v, cache_logf, state_pool, page_table, meta_tokens,
              ln_ffn1_pre, ln_ffn1_post, ffn1_wi, ffn1_wo, ln_mix_pre, ln_mix_post, w_in, b_forget,
              pool_w_grp, pool_scale, w_pool_up, w_att_up, w_out, ln_ffn2_pre, ln_ffn2_post,
              ffn2_wi, ffn2_wo):
    B = x_prompt.shape[0]
    DB, T = x_sample.shape[:2]
    n_pages = page_table.shape[1]
    past_len = n_pages * PAGE_SIZE
    meta = jnp.broadcast_to(meta_tokens.astype(x_prompt.dtype)[None], (B, N_META, D_MODEL))
    xp = jnp.concatenate([meta, x_prompt], axis=1)
    xs = x_sample
    pos_k_s = jnp.arange(past_len + T)
    pos_q_s = past_len + jnp.arange(T)
    kp_l, vp_l, fp_l, pp_l, ks_l, vs_l, fs_l, ps_l = [], [], [], [], [], [], [], []
    for l in range(DEPTH):
        xp = ffn_half_step(xp, ln_ffn1_pre[l], ln_ffn1_post[l], ffn1_wi[l], ffn1_wo[l])
        xs = ffn_half_step(xs, ln_ffn1_pre[l], ln_ffn1_post[l], ffn1_wi[l], ffn1_wo[l])

        h = rmsnorm(xp, ln_mix_pre[l])
        u, q, k, v, logf, gp, ga = project_in(h, w_in[l], b_forget[l])
        pu = pool_mix(u, 0, 0, pool_w_grp[l], pool_scale[l])
        c = jnp.cumsum(logf, axis=1)
        ao = prompt_attention(q, k, v, c)
        xp = xp + rmsnorm(merge(pu, ao, gp, ga, w_pool_up[l], w_att_up[l], w_out[l]), ln_mix_post[l])
        kp_l.append(k); vp_l.append(v); fp_l.append(logf.astype(cache_logf.dtype)); pp_l.append(u[:, -POOL_HIST:])

        h = rmsnorm(xs, ln_mix_pre[l])
        u, q, k, v, logf, gp, ga = project_in(h, w_in[l], b_forget[l])
        u_ext = jnp.concatenate([state_pool[l].astype(u.dtype), u], axis=1)
        pu = pool_mix(u_ext, POOL_HIST, past_len, pool_w_grp[l], pool_scale[l])
        k_past = cache_k[l, page_table].reshape(DB, past_len, N_HEADS, HEAD_DIM)
        v_past = cache_v[l, page_table].reshape(DB, past_len, N_HEADS, HEAD_DIM)
        f_past = cache_logf[l, page_table].reshape(DB, past_len, N_HEADS)
        k_all = jnp.concatenate([k_past.astype(k.dtype), k], axis=1)
        v_all = jnp.concatenate([v_past.astype(v.dtype), v], axis=1)
        c_all = jnp.cumsum(jnp.concatenate([f_past.astype(jnp.float32), logf], axis=1), axis=1)
        ao = fox_attend(q, k_all, v_all, c_all[:, past_len:], c_all, pos_q_s, pos_k_s)
        xs = xs + rmsnorm(merge(pu, ao, gp, ga, w_pool_up[l], w_att_up[l], w_out[l]), ln_mix_post[l])
        ks_l.append(k); vs_l.append(v); fs_l.append(logf.astype(cache_logf.dtype)); ps_l.append(u_ext[:, -POOL_HIST:])

        xp = ffn_half_step(xp, ln_ffn2_pre[l], ln_ffn2_post[l], ffn2_wi[l], ffn2_wo[l])
        xs = ffn_half_step(xs, ln_ffn2_pre[l], ln_ffn2_post[l], ffn2_wi[l], ffn2_wo[l])

    y_prompt = xp[:, N_META:]
    y_sample = xs
    return (y_prompt, y_sample,
            jnp.stack(kp_l), jnp.stack(vp_l), jnp.stack(fp_l), jnp.stack(pp_l),
            jnp.stack(ks_l), jnp.stack(vs_l), jnp.stack(fs_l), jnp.stack(ps_l))
```

```python
import functools

import jax
import jax.numpy as jnp
from jax import lax
from jax.experimental import pallas as pl
from jax.experimental.pallas import tpu as pltpu

F32 = jnp.float32
BF16 = jnp.bfloat16

RMS_EPS = 1e-6
N_META = 16
N_HEADS = 8
HEAD_DIM = 64
HEAD_SHIFT = 6
TOK_SHIFT = 3
ATT_W = N_HEADS * HEAD_DIM
POOL_WINDOWS = (2, 4, 8, 16)
POOL_GROUP_W = 128
POOL_W = len(POOL_WINDOWS) * POOL_GROUP_W
POOL_HIST = max(POOL_WINDOWS) - 1
HIST_ROWS = 16
PAGE = 128
LANES = 128
NEG = float(jnp.finfo(jnp.float32).min)

ATT_BLOCK = 256
DEC_PAGES_PER_STEP = 8
VMEM_LIMIT = 56 * 1024 * 1024


def _cparams(*sem):
    return pltpu.CompilerParams(dimension_semantics=sem, vmem_limit_bytes=VMEM_LIMIT)


def _rms(x, g):
    ms = jnp.mean(x * x, axis=-1, keepdims=True)
    return x * lax.rsqrt(ms + RMS_EPS) * g


def _log_sigmoid(x):
    return -(jnp.maximum(-x, 0.0) + jnp.log1p(jnp.exp(-jnp.abs(x))))


def _split3(x):
    a = x.astype(BF16).astype(F32)
    r = x - a
    b = r.astype(BF16).astype(F32)
    c = (r - b).astype(BF16).astype(F32)
    return a, b, c


def _dot(a, b):
    return jnp.dot(a, b, preferred_element_type=F32)


def _dot_nt(a, b):
    return lax.dot_general(a, b, (((1,), (1,)), ((), ())), preferred_element_type=F32)


def _ffn_body(x_ref, gpre_ref, gpost_ref, wig_ref, wiu_ref, wo_ref, o_ref, a_ref, *, chunk):
    x = x_ref[...]
    h = _rms(x, gpre_ref[...]).astype(BF16)
    d_ff = a_ref.shape[1]
    for c in range(d_ff // chunk):
        sl = slice(c * chunk, (c + 1) * chunk)
        g = _dot(h, wig_ref[:, sl])
        u = _dot(h, wiu_ref[:, sl])
        a_ref[:, sl] = (g * jax.nn.sigmoid(g) * u).astype(BF16)
    y = _dot(a_ref[...], wo_ref[...])
    o_ref[...] = x + 0.5 * _rms(y, gpost_ref[...])


def _ffn(x, g_pre, g_post, wi, wo, layer, tm):
    n, d = x.shape
    d_ff = wo.shape[1]
    const = lambda i: (layer, 0, 0)
    return pl.pallas_call(
        functools.partial(_ffn_body, chunk=256),
        grid=(n // tm,),
        in_specs=[
            pl.BlockSpec((tm, d), lambda i: (i, 0)),
            pl.BlockSpec((None, 1, d), const),
            pl.BlockSpec((None, 1, d), const),
            pl.BlockSpec((None, d, d_ff), const),
            pl.BlockSpec((None, d, d_ff), lambda i: (layer, 0, 1)),
            pl.BlockSpec((None, d_ff, d), const),
        ],
        out_specs=pl.BlockSpec((tm, d), lambda i: (i, 0)),
        out_shape=jax.ShapeDtypeStruct((n, d), F32),
        scratch_shapes=[pltpu.VMEM((tm, d_ff), BF16)],
        compiler_params=_cparams("arbitrary"),
        name="ffn_half_step",
    )(x, g_pre, g_post, wi, wi, wo)


def _inproj_rows_body(x_ref, g_ref, w_ref, bf_ref, u_ref, q_ref, sg_ref, *rest, with_kv):
    x = x_ref[...]
    h = _rms(x, g_ref[...]).astype(BF16)
    u_ref[...] = _dot(h, w_ref[:, 0:POOL_W])
    q_ref[...] = (_dot(h, w_ref[:, POOL_W:POOL_W + ATT_W]) * (HEAD_DIM ** -0.5)).astype(q_ref.dtype)
    g0 = POOL_W + ATT_W
    d = x.shape[1]
    for c in range(2 * d // 512):
        sg_ref[:, c * 512:(c + 1) * 512] = jax.nn.sigmoid(_dot(h, w_ref[:, g0 + c * 512:g0 + (c + 1) * 512]))
    if with_kv:
        k_ref, v_ref, lf_ref = rest
        k0 = g0 + 2 * d
        k_ref[...] = _dot(h, w_ref[:, k0:k0 + ATT_W])
        v_ref[...] = _dot(h, w_ref[:, k0 + ATT_W:k0 + 2 * ATT_W])
        zf = _dot(h, w_ref[:, k0 + 2 * ATT_W:k0 + 2 * ATT_W + LANES])
        lf_ref[...] = _log_sigmoid(zf + bf_ref[...])


def _inproj_rows(x, g, w, bf, layer, tm, with_kv):
    n, d = x.shape
    wcols = w.shape[2]
    const = lambda i: (layer, 0, 0)
    row = lambda i: (i, 0)
    out_shape = [jax.ShapeDtypeStruct((n, POOL_W), F32),
                 jax.ShapeDtypeStruct((n, ATT_W), BF16 if not with_kv else F32),
                 jax.ShapeDtypeStruct((n, 2 * d), F32)]
    out_specs = [pl.BlockSpec((tm, POOL_W), row), pl.BlockSpec((tm, ATT_W), row), pl.BlockSpec((tm, 2 * d), row)]
    if with_kv:
        out_shape += [jax.ShapeDtypeStruct((n, ATT_W), F32), jax.ShapeDtypeStruct((n, ATT_W), F32),
                      jax.ShapeDtypeStruct((n, LANES), F32)]
        out_specs += [pl.BlockSpec((tm, ATT_W), row), pl.BlockSpec((tm, ATT_W), row), pl.BlockSpec((tm, LANES), row)]
    return pl.pallas_call(
        functools.partial(_inproj_rows_body, with_kv=with_kv),
        grid=(n // tm,),
        in_specs=[
            pl.BlockSpec((tm, d), row),
            pl.BlockSpec((None, 1, d), const),
            pl.BlockSpec((None, d, wcols), const),
            pl.BlockSpec((None, 1, LANES), const),
        ],
        out_specs=out_specs,
        out_shape=out_shape,
        compiler_params=_cparams("arbitrary"),
        name="inproj_rows_kv" if with_kv else "inproj_rows",
    )(x, g, w, bf)


def _inproj_t_body(x_ref, g_ref, wt_ref, wft_ref, bft_ref, tri_ref, kt_ref, vt_ref, lft_ref, ct_ref,
                   carry_ref, *, seq_len):
    j = pl.program_id(1)
    tl = x_ref.shape[1]
    h = _rms(x_ref[0], g_ref[...]).astype(BF16)
    zt = _dot_nt(wt_ref[...], h)
    kt_ref[0] = zt[0:ATT_W]
    vt_ref[0] = zt[ATT_W:2 * ATT_W]
    zf = _dot_nt(wft_ref[...], h)[0:N_HEADS]
    lf = _log_sigmoid(zf + bft_ref[...])
    lft_ref[0] = lf

    @pl.when(j == 0)
    def _():
        carry_ref[...] = jnp.zeros_like(carry_ref)

    pos = j * tl + lax.broadcasted_iota(jnp.int32, lf.shape, 1)
    lf = jnp.where(pos < seq_len, lf, 0.0)
    a, b, c = _split3(lf)
    stack = jnp.concatenate([a, b, c, jnp.zeros_like(a)], axis=0).astype(BF16)
    cs = _dot(stack, tri_ref[...])
    run = cs[0:8] + cs[8:16] + cs[16:24] + carry_ref[...]
    ct_ref[0] = run
    carry_ref[...] = run[:, tl - 1:tl]


def _inproj_t(xp, g, wt, wft, bft, layer, tl):
    b, seq_len, d = xp.shape
    nt = pl.cdiv(seq_len, tl)
    tri = (lax.broadcasted_iota(jnp.int32, (tl, tl), 0) <= lax.broadcasted_iota(jnp.int32, (tl, tl), 1)).astype(BF16)
    const = lambda i, j: (layer, 0, 0)
    tile = lambda i, j: (i, 0, j)
    return pl.pallas_call(
        functools.partial(_inproj_t_body, seq_len=seq_len),
        grid=(b, nt),
        in_specs=[
            pl.BlockSpec((1, tl, d), lambda i, j: (i, j, 0)),
            pl.BlockSpec((None, 1, d), const),
            pl.BlockSpec((None, 2 * ATT_W, d), const),
            pl.BlockSpec((None, 16, d), const),
            pl.BlockSpec((None, N_HEADS, 1), const),
            pl.BlockSpec((tl, tl), lambda i, j: (0, 0)),
        ],
        out_specs=[
            pl.BlockSpec((1, ATT_W, tl), tile),
            pl.BlockSpec((1, ATT_W, tl), tile),
            pl.BlockSpec((1, N_HEADS, tl), tile),
            pl.BlockSpec((1, N_HEADS, tl), tile),
        ],
        out_shape=[
            jax.ShapeDtypeStruct((b, ATT_W, seq_len), F32),
            jax.ShapeDtypeStruct((b, ATT_W, seq_len), F32),
            jax.ShapeDtypeStruct((b, N_HEADS, seq_len), F32),
            jax.ShapeDtypeStruct((b, N_HEADS, seq_len), F32),
        ],
        scratch_shapes=[pltpu.VMEM((N_HEADS, 1), F32)],
        compiler_params=_cparams("arbitrary", "arbitrary"),
        name="inproj_feature_major",
    )(xp, g, wt, wft, bft, tri)


def _pool_mix_group(tot, cur, cnt, w_ref, scale_ref, g):
    cs = slice(g * POOL_GROUP_W, (g + 1) * POOL_GROUP_W)
    d = tot / cnt - cur
    return _dot(d.astype(BF16), w_ref[g]) * scale_ref[:, cs]


def _pool_seq_body(u_ref, w_ref, scale_ref, o_ref, pad_ref, *, chunk):
    seq_len = u_ref.shape[1]
    pad_ref[0:HIST_ROWS, :] = jnp.zeros((HIST_ROWS, POOL_W), F32)
    pad_ref[HIST_ROWS:HIST_ROWS + seq_len, :] = u_ref[0]
    for r0 in range(0, seq_len, chunk):
        pos = r0 + lax.broadcasted_iota(jnp.int32, (chunk, 1), 0)
        for g, w in enumerate(POOL_WINDOWS):
            cs = slice(g * POOL_GROUP_W, (g + 1) * POOL_GROUP_W)
            base = HIST_ROWS + r0
            cur = pad_ref[base:base + chunk, cs]
            tot = cur
            for i in range(1, w):
                tot = tot + pad_ref[base - i:base - i + chunk, cs]
            cnt = jnp.minimum(w, pos + 1).astype(F32)
            o_ref[0, r0:r0 + chunk, cs] = _pool_mix_group(tot, cur, cnt, w_ref, scale_ref, g).astype(o_ref.dtype)


def _pool_seq(u, w_grp, scale, layer):
    b, seq_len, _ = u.shape
    chunk = seq_len // 3
    const3 = lambda i: (layer, 0, 0)
    return pl.pallas_call(
        functools.partial(_pool_seq_body, chunk=chunk),
        grid=(b,),
        in_specs=[
            pl.BlockSpec((1, seq_len, POOL_W), lambda i: (i, 0, 0)),
            pl.BlockSpec((None, len(POOL_WINDOWS), POOL_GROUP_W, POOL_GROUP_W), lambda i: (layer, 0, 0, 0)),
            pl.BlockSpec((None, 1, POOL_W), const3),
        ],
        out_specs=pl.BlockSpec((1, seq_len, POOL_W), lambda i: (i, 0, 0)),
        out_shape=jax.ShapeDtypeStruct((b, seq_len, POOL_W), BF16),
        scratch_shapes=[pltpu.VMEM((HIST_ROWS + seq_len, POOL_W), F32)],
        compiler_params=_cparams("arbitrary"),
        name="pool_prompt",
    )(u, w_grp, scale)


def _pool_step_body(hist_ref, u_ref, w_ref, scale_ref, o_ref, *, pos0):
    n_hist = hist_ref.shape[1]
    t_new = u_ref.shape[0]

    def slab(i):
        return hist_ref[0, i] if i < n_hist else u_ref[i - n_hist]

    for t in range(t_new):
        end = n_hist + t
        for g, w in enumerate(POOL_WINDOWS):
            cs = slice(g * POOL_GROUP_W, (g + 1) * POOL_GROUP_W)
            cur = slab(end)[:, cs]
            tot = cur
            for i in range(1, w):
                if end - i >= 0:
                    tot = tot + slab(end - i)[:, cs]
            cnt = float(min(w, pos0 + t + 1))
            o_ref[t, :, cs] = _pool_mix_group(tot, cur, cnt, w_ref, scale_ref, g).astype(o_ref.dtype)


def _pool_step(hist, u_new, w_grp, scale, layer, pos0):
    _, n_hist, nb, _ = hist.shape
    t_new = u_new.shape[0]
    return pl.pallas_call(
        functools.partial(_pool_step_body, pos0=pos0),
        grid=(1,),
        in_specs=[
            pl.BlockSpec((1, n_hist, nb, POOL_W), lambda i: (layer, 0, 0, 0)),
            pl.BlockSpec((t_new, nb, POOL_W), lambda i: (0, 0, 0)),
            pl.BlockSpec((None, len(POOL_WINDOWS), POOL_GROUP_W, POOL_GROUP_W), lambda i: (layer, 0, 0, 0)),
            pl.BlockSpec((None, 1, POOL_W), lambda i: (layer, 0, 0)),
        ],
        out_specs=pl.BlockSpec((t_new, nb, POOL_W), lambda i: (0, 0, 0)),
        out_shape=jax.ShapeDtypeStruct((t_new, nb, POOL_W), F32),
        compiler_params=_cparams("arbitrary"),
        name="pool_sample",
    )(hist, u_new, w_grp, scale)


def _attn_body(q_ref, kt_ref, vt_ref, ct_ref, cc_ref, o_ref, qa_ref, ka_ref, va_ref, oacc_ref, *, blk):
    hp = pl.program_id(1)
    seq_len = q_ref.shape[1]
    nblk = qa_ref.shape[1]
    pad_len = nblk * blk
    f32min = NEG

    lane_q = lax.broadcasted_iota(jnp.int32, (seq_len, LANES), 1)
    row_k = lax.broadcasted_iota(jnp.int32, (LANES, seq_len), 0)
    lane_h = lax.broadcasted_iota(jnp.int32, (seq_len, N_HEADS), 1)
    qpair = q_ref[0].astype(F32)
    kpair = kt_ref[0]
    n_full = seq_len // blk
    tail = seq_len - n_full * blk
    assert nblk == n_full + 1 and tail % 16 == 0

    def fill_cols(dst, val):
        for j in range(n_full):
            dst[j] = val[:, j * blk:(j + 1) * blk]
        dst[n_full] = jnp.zeros(dst.shape[1:], dst.dtype)
        dst[n_full, :, 0:tail] = val[:, n_full * blk:seq_len]

    fill_cols(va_ref, vt_ref[0].astype(BF16))

    for hh in range(2):
        head = 2 * hp + hh
        f0 = HEAD_DIM * (1 - hh)
        ccol = jnp.sum(jnp.where(lane_h == head, cc_ref[0], 0.0), axis=1, keepdims=True)
        crow = ct_ref[0, pl.ds(head, 1), :]
        q1, q2, q3 = _split3(ccol)
        k1, k2, k3 = _split3(crow)
        in_head_q = (lane_q >= HEAD_DIM * hh) & (lane_q < HEAD_DIM * (hh + 1))
        qa = jnp.where(in_head_q, qpair,
             jnp.where(lane_q == f0, q1, jnp.where(lane_q == f0 + 1, q2, jnp.where(lane_q == f0 + 2, q3,
             jnp.where((lane_q >= f0 + 3) & (lane_q < f0 + 6), 1.0, 0.0)))))
        in_head_k = (row_k >= HEAD_DIM * hh) & (row_k < HEAD_DIM * (hh + 1))
        ka = jnp.where(in_head_k, kpair,
             jnp.where((row_k >= f0) & (row_k < f0 + 3), 1.0,
             jnp.where(row_k == f0 + 3, -k1, jnp.where(row_k == f0 + 4, -k2, jnp.where(row_k == f0 + 5, -k3, 0.0)))))
        qa = qa.astype(BF16)
        for j in range(n_full):
            qa_ref[hh, j] = qa[j * blk:(j + 1) * blk, :]
        qa_ref[hh, n_full] = jnp.zeros((blk, LANES), BF16)
        qa_ref[hh, n_full, 0:tail, :] = qa[n_full * blk:seq_len, :]
        fill_cols(ka_ref.at[hh], ka.astype(BF16))

    diag_ok = lax.broadcasted_iota(jnp.int32, (blk, blk), 1) <= lax.broadcasted_iota(jnp.int32, (blk, blk), 0)
    lane_o = lax.broadcasted_iota(jnp.int32, (blk, LANES), 1)

    def update(state, s, j):
        m, l, acc = state
        m_new = jnp.maximum(m, jnp.max(s, axis=1, keepdims=True))
        alpha = jnp.exp(m - m_new)
        p = jnp.exp(s - m_new)
        l = alpha * l + jnp.sum(p, axis=1, keepdims=True)
        acc = alpha * acc + _dot_nt(p.astype(BF16), va_ref[j])
        return m_new, l, acc

    for hh in range(2):
        def q_block(i, carry, hh=hh):
            qb = qa_ref[hh, i]

            def kv_block(j, state):
                return update(state, _dot(qb, ka_ref[hh, j]), j)

            init = (jnp.full((blk, 1), f32min, F32), jnp.zeros((blk, 1), F32), jnp.zeros((blk, LANES), F32))
            state = lax.fori_loop(0, i, kv_block, init)
            s = jnp.where(diag_ok, _dot(qb, ka_ref[hh, i]), f32min)
            _, l, acc = update(state, s, i)
            out = acc / l
            rows = pl.ds(pl.multiple_of(i * blk, blk), blk)
            if hh == 0:
                oacc_ref[rows, :] = out
            else:
                oacc_ref[rows, :] = jnp.where(lane_o >= HEAD_DIM, out, oacc_ref[rows, :])
            return carry

        lax.fori_loop(0, nblk, q_block, 0)

    o_ref[0] = oacc_ref[0:seq_len, :].astype(o_ref.dtype)


def _attn_prompt(q, kt, vt, ct, cc):
    b, seq_len, _ = q.shape
    blk = ATT_BLOCK
    nblk = pl.cdiv(seq_len, blk)
    return pl.pallas_call(
        functools.partial(_attn_body, blk=blk),
        grid=(b, N_HEADS // 2),
        in_specs=[
            pl.BlockSpec((1, seq_len, LANES), lambda i, p: (i, 0, p)),
            pl.BlockSpec((1, LANES, seq_len), lambda i, p: (i, p, 0)),
            pl.BlockSpec((1, LANES, seq_len), lambda i, p: (i, p, 0)),
            pl.BlockSpec((1, N_HEADS, seq_len), lambda i, p: (i, 0, 0)),
            pl.BlockSpec((1, seq_len, N_HEADS), lambda i, p: (i, 0, 0)),
        ],
        out_specs=pl.BlockSpec((1, seq_len, LANES), lambda i, p: (i, 0, p)),
        out_shape=jax.ShapeDtypeStruct((b, seq_len, ATT_W), BF16),
        scratch_shapes=[
            pltpu.VMEM((2, nblk, blk, LANES), BF16),
            pltpu.VMEM((2, nblk, LANES, blk), BF16),
            pltpu.VMEM((nblk, LANES, blk), BF16),
            pltpu.VMEM((nblk * blk, LANES), F32),
        ],
        compiler_params=_cparams("arbitrary", "arbitrary"),
        name="attn_prompt",
    )(q, kt, vt, ct, cc)


def _decode_body(pt_ref, q_ref, kn_ref, vn_ref, lfn_ref, *rest, n_pg, t_new):
    k_refs = rest[0:n_pg]
    v_refs = rest[n_pg:2 * n_pg]
    f_refs = rest[2 * n_pg:3 * n_pg]
    o_ref = rest[3 * n_pg]
    qbd_ref, m_ref, l_ref, acc_ref, carry_ref, cnew_ref = rest[3 * n_pg + 1:]
    j = pl.program_id(1)
    n_rows = t_new * N_HEADS

    row_h = lax.broadcasted_iota(jnp.int32, (n_rows, ATT_W), 0) & (N_HEADS - 1)
    own = (lax.broadcasted_iota(jnp.int32, (n_rows, ATT_W), 1) >> HEAD_SHIFT) == row_h

    r_i = lax.broadcasted_iota(jnp.int32, (n_rows, n_rows), 0)
    c_i = lax.broadcasted_iota(jnp.int32, (n_rows, n_rows), 1)
    same_head = (r_i & (N_HEADS - 1)) == (c_i & (N_HEADS - 1))

    def new_token_sum(lo):
        sel = same_head & ((c_i >> TOK_SHIFT) > lo) & ((c_i >> TOK_SHIFT) <= (r_i >> TOK_SHIFT))
        return jnp.sum(jnp.where(sel, lfn_ref[0], 0.0), axis=1, keepdims=True)

    @pl.when(j == 0)
    def _():
        q4 = q_ref[0]
        rep = jnp.concatenate([jnp.broadcast_to(q4[t:t + 1], (N_HEADS, ATT_W)) for t in range(t_new)], axis=0)
        qbd_ref[...] = jnp.where(own, rep, 0.0).astype(BF16)
        m_ref[...] = jnp.full_like(m_ref, NEG)
        l_ref[...] = jnp.zeros_like(l_ref)
        acc_ref[...] = jnp.zeros_like(acc_ref)
        carry_ref[...] = jnp.zeros_like(carry_ref)
        cnew_ref[...] = new_token_sum(-1)

    def update(s, pv):
        m = m_ref[...]
        m_new = jnp.maximum(m, jnp.max(s, axis=1, keepdims=True))
        alpha = jnp.exp(m - m_new)
        p = jnp.exp(s - m_new)
        l_ref[...] = alpha * l_ref[...] + jnp.sum(p, axis=1, keepdims=True)
        acc_ref[...] = alpha * acc_ref[...] + pv(p.astype(BF16))
        m_ref[...] = m_new

    later = (lax.broadcasted_iota(jnp.int32, (PAGE, PAGE), 0) >
             lax.broadcasted_iota(jnp.int32, (PAGE, PAGE), 1)).astype(BF16)
    qbd = qbd_ref[...]
    for p in range(n_pg):
        kt = k_refs[p][...].reshape(ATT_W, PAGE).astype(BF16)
        vt = v_refs[p][...].reshape(ATT_W, PAGE).astype(BF16)
        f = f_refs[p][...]
        a, b, c = _split3(f)
        stack = jnp.concatenate([a, b, c, jnp.zeros_like(a)], axis=0).astype(BF16)
        cs = _dot(stack, later)
        suffix = cs[0:8] + cs[8:16] + cs[16:24] + carry_ref[...]
        carry_ref[...] = suffix[:, 0:1] + f[:, 0:1]
        bias = jnp.concatenate([suffix] * t_new, axis=0) + cnew_ref[...]
        s = _dot(qbd, kt) + bias
        update(s, lambda pb, vt=vt: _dot_nt(pb, vt))

    @pl.when(j == pl.num_programs(1) - 1)
    def _():
        zpad = jnp.zeros((PAGE - t_new, ATT_W), F32)
        kpad = jnp.concatenate([kn_ref[0], zpad], axis=0).astype(BF16)
        vpad = jnp.concatenate([vn_ref[0], zpad], axis=0).astype(BF16)
        lane = lax.broadcasted_iota(jnp.int32, (n_rows, PAGE), 1)
        tok = lax.broadcasted_iota(jnp.int32, (n_rows, PAGE), 0) >> TOK_SHIFT
        bias = jnp.zeros((n_rows, PAGE), F32)
        for t in range(t_new):
            bias = jnp.where(lane == t, new_token_sum(t), bias)
        s = jnp.where(lane <= tok, _dot_nt(qbd, kpad) + bias, NEG)
        update(s, lambda pb: _dot(pb, vpad))
        out = jnp.where(own, acc_ref[...] / l_ref[...], 0.0)
        o_ref[0] = jnp.sum(out.reshape(t_new, N_HEADS, ATT_W), axis=1)


def _decode(page_table, q, k_new, v_new, lf_new, ckt, cvt, cft, layer):
    nb, t_new, _ = q.shape
    n_pages = page_table.shape[1]
    n_pg = DEC_PAGES_PER_STEP
    n_rows = t_new * N_HEADS

    def page_spec(p, shape):
        zeros = (0,) * len(shape)

        def index(b, j, pt):
            return (layer, pt[b, n_pages - 1 - (j * n_pg + p)]) + zeros

        return pl.BlockSpec((None, None) + shape, index)

    new = lambda b, j, pt: (b, 0, 0)
    in_specs = [pl.BlockSpec((1, t_new, ATT_W), new), pl.BlockSpec((1, t_new, ATT_W), new),
                pl.BlockSpec((1, t_new, ATT_W), new), pl.BlockSpec((1, 1, n_rows), new)]
    in_specs += [page_spec(p, (N_HEADS, HEAD_DIM, PAGE)) for p in range(n_pg)]
    in_specs += [page_spec(p, (N_HEADS, HEAD_DIM, PAGE)) for p in range(n_pg)]
    in_specs += [page_spec(p, (N_HEADS, PAGE)) for p in range(n_pg)]
    grid_spec = pltpu.PrefetchScalarGridSpec(
        num_scalar_prefetch=1,
        grid=(nb, n_pages // n_pg),
        in_specs=in_specs,
        out_specs=pl.BlockSpec((1, t_new, ATT_W), new),
        scratch_shapes=[
            pltpu.VMEM((n_rows, ATT_W), BF16),
            pltpu.VMEM((n_rows, 1), F32),
            pltpu.VMEM((n_rows, 1), F32),
            pltpu.VMEM((n_rows, ATT_W), F32),
            pltpu.VMEM((N_HEADS, 1), F32),
            pltpu.VMEM((n_rows, 1), F32),
        ],
    )
    return pl.pallas_call(
        functools.partial(_decode_body, n_pg=n_pg, t_new=t_new),
        grid_spec=grid_spec,
        out_shape=jax.ShapeDtypeStruct((nb, t_new, ATT_W), F32),
        compiler_params=_cparams("arbitrary", "arbitrary"),
        name="attn_decode",
    )(page_table, q, k_new, v_new, lf_new, *([ckt] * n_pg), *([cvt] * n_pg), *([cft] * n_pg))


def _merge_body(x_ref, pu_ref, ao_ref, sg_ref, wpu_ref, wau_ref, wout_ref, gpost_ref, o_ref):
    d = x_ref.shape[1]
    m = (sg_ref[:, 0:d] * _dot(pu_ref[...].astype(BF16), wpu_ref[...])
         + sg_ref[:, d:2 * d] * _dot(ao_ref[...].astype(BF16), wau_ref[...]))
    y = _dot(m.astype(BF16), wout_ref[...])
    o_ref[...] = x_ref[...] + _rms(y, gpost_ref[...])


def _merge(x, pu, ao, sg, wpu, wau, wout, g_post, layer, tm):
    n, d = x.shape
    row = lambda i: (i, 0)
    const = lambda i: (layer, 0, 0)
    return pl.pallas_call(
        _merge_body,
        grid=(n // tm,),
        in_specs=[
            pl.BlockSpec((tm, d), row),
            pl.BlockSpec((tm, POOL_W), row),
            pl.BlockSpec((tm, ATT_W), row),
            pl.BlockSpec((tm, 2 * d), row),
            pl.BlockSpec((None, POOL_W, d), const),
            pl.BlockSpec((None, ATT_W, d), const),
            pl.BlockSpec((None, d, d), const),
            pl.BlockSpec((None, 1, d), const),
        ],
        out_specs=pl.BlockSpec((tm, d), row),
        out_shape=jax.ShapeDtypeStruct((n, d), F32),
        compiler_params=_cparams("arbitrary"),
        name="merge",
    )(x, pu, ao, sg, wpu, wau, wout, g_post)


def _row_tile(n, cap):
    best = None
    for t in range(16, cap + 1, 16):
        if n % t == 0:
            best = t
    assert best is not None, n
    return best


def kernel(x_prompt, x_sample, cache_k, cache_v, cache_logf, state_pool, page_table, meta_tokens,
           ln_ffn1_pre, ln_ffn1_post, ffn1_wi, ffn1_wo, ln_mix_pre, ln_mix_post, w_in, b_forget,
           pool_w_grp, pool_scale, w_pool_up, w_att_up, w_out, ln_ffn2_pre, ln_ffn2_post,
           ffn2_wi, ffn2_wo):
    nbp, seq, d = x_prompt.shape
    nbs, t_new, _ = x_sample.shape
    depth = w_in.shape[0]
    n_pages = page_table.shape[1]
    past_len = n_pages * PAGE
    seq_len = seq + N_META

    meta = jnp.broadcast_to(meta_tokens.astype(x_prompt.dtype)[None], (nbp, N_META, d))
    xp = jnp.concatenate([meta, x_prompt], axis=1).reshape(nbp * seq_len, d)
    xs = x_sample.reshape(nbs * t_new, d)
    tm_p = _row_tile(nbp * seq_len, 768)
    tm_s = nbs * t_new

    bf = lambda w: w.astype(BF16)
    vec = lambda g: g[:, None, :]
    o1, o2, o3, o4, o5 = POOL_W, POOL_W + ATT_W, POOL_W + 2 * ATT_W, POOL_W + 3 * ATT_W, POOL_W + 3 * ATT_W + N_HEADS
    w_in_b = bf(w_in)
    w_f_pad = jnp.pad(w_in_b[:, :, o4:o5], ((0, 0), (0, 0), (0, LANES - N_HEADS)))
    w_rows_p = jnp.concatenate([w_in_b[:, :, 0:o2], w_in_b[:, :, o5:]], axis=2)
    w_rows_s = jnp.concatenate([w_rows_p, w_in_b[:, :, o2:o4], w_f_pad], axis=2)
    w_kv_t = jnp.transpose(w_in_b[:, :, o2:o4], (0, 2, 1))
    w_f_t = jnp.pad(jnp.transpose(w_in_b[:, :, o4:o5], (0, 2, 1)), ((0, 0), (0, 16 - N_HEADS), (0, 0)))
    bf_row = jnp.pad(b_forget, ((0, 0), (0, LANES - N_HEADS)))[:, None, :]
    bf_col = b_forget[:, :, None]
    ffn1_wi_b, ffn1_wo_b, ffn2_wi_b, ffn2_wo_b = bf(ffn1_wi), bf(ffn1_wo), bf(ffn2_wi), bf(ffn2_wo)
    w_grp_b, wpu_b, wau_b, wout_b = bf(pool_w_grp), bf(w_pool_up), bf(w_att_up), bf(w_out)
    scale3 = vec(pool_scale)
    g1pre, g1post, gmpre, gmpost, g2pre, g2post = (vec(g) for g in (
        ln_ffn1_pre, ln_ffn1_post, ln_mix_pre, ln_mix_post, ln_ffn2_pre, ln_ffn2_post))

    ckt = jnp.transpose(cache_k, (0, 1, 3, 4, 2))
    cvt = jnp.transpose(cache_v, (0, 1, 3, 4, 2))
    cft = jnp.transpose(cache_logf, (0, 1, 3, 2))
    hist = jnp.transpose(state_pool, (0, 2, 1, 3))

    kp_l, vp_l, fp_l, pp_l, ks_l, vs_l, fs_l, ps_l = [], [], [], [], [], [], [], []
    for l in range(depth):
        xp = _ffn(xp, g1pre, g1post, ffn1_wi_b, ffn1_wo_b, l, tm_p)
        xs = _ffn(xs, g1pre, g1post, ffn1_wi_b, ffn1_wo_b, l, tm_s)

        u, q, sg = _inproj_rows(xp, gmpre, w_rows_p, bf_row, l, tm_p, with_kv=False)
        kt, vt, lft, ct = _inproj_t(xp.reshape(nbp, seq_len, d), gmpre, w_kv_t, w_f_t, bf_col, l, 256)
        u3 = u.reshape(nbp, seq_len, POOL_W)
        pu = _pool_seq(u3, w_grp_b, scale3, l)
        ao = _attn_prompt(q.reshape(nbp, seq_len, ATT_W), kt, vt, ct, jnp.transpose(ct, (0, 2, 1)))
        xp = _merge(xp, pu.reshape(-1, POOL_W), ao.reshape(-1, ATT_W), sg, wpu_b, wau_b, wout_b, gmpost, l, tm_p)
        kp_l.append(kt); vp_l.append(vt); fp_l.append(lft); pp_l.append(u3[:, seq_len - POOL_HIST:])

        u, q, sg, k, v, lf = _inproj_rows(xs, gmpre, w_rows_s, bf_row, l, tm_s, with_kv=True)
        u_tb = jnp.transpose(u.reshape(nbs, t_new, POOL_W), (1, 0, 2))
        pu = jnp.transpose(_pool_step(hist, u_tb, w_grp_b, scale3, l, past_len), (1, 0, 2))
        lf8 = lf[:, 0:N_HEADS].reshape(nbs, t_new, N_HEADS)
        ao = _decode(page_table, q.reshape(nbs, t_new, ATT_W), k.reshape(nbs, t_new, ATT_W),
                     v.reshape(nbs, t_new, ATT_W), lf8.reshape(nbs, 1, t_new * N_HEADS), ckt, cvt, cft, l)
        xs = _merge(xs, pu.reshape(-1, POOL_W), ao.reshape(-1, ATT_W), sg, wpu_b, wau_b, wout_b, gmpost, l, tm_s)
        ks_l.append(k.reshape(nbs, t_new, N_HEADS, HEAD_DIM)); vs_l.append(v.reshape(nbs, t_new, N_HEADS, HEAD_DIM))
        fs_l.append(lf8)
        ps_l.append(jnp.concatenate([state_pool[l], u.reshape(nbs, t_new, POOL_W)], axis=1)[:, t_new:])

        xp = _ffn(xp, g2pre, g2post, ffn2_wi_b, ffn2_wo_b, l, tm_p)
        xs = _ffn(xs, g2pre, g2post, ffn2_wi_b, ffn2_wo_b, l, tm_s)

    y_prompt = xp.reshape(nbp, seq_len, d)[:, N_META:]
    y_sample = xs.reshape(nbs, t_new, d)
    heads_t = lambda a: jnp.transpose(jnp.stack(a).reshape(depth, nbp, N_HEADS, HEAD_DIM, seq_len), (0, 1, 4, 2, 3))
    return (y_prompt, y_sample,
            heads_t(kp_l), heads_t(vp_l), jnp.transpose(jnp.stack(fp_l), (0, 1, 3, 2)), jnp.stack(pp_l),
            jnp.stack(ks_l), jnp.stack(vs_l), jnp.stack(fs_l), jnp.stack(ps_l))
```

```python
import functools

import jax
import jax.numpy as jnp
from jax import lax
from jax.experimental import pallas as pl
from jax.experimental.pallas import tpu as pltpu

F32 = jnp.float32
BF16 = jnp.bfloat16

RMS_EPS = 1e-6
N_META = 16
N_HEADS = 8
HEAD_DIM = 64
HEAD_SHIFT = 6
TOK_SHIFT = 3
ATT_W = N_HEADS * HEAD_DIM
POOL_WINDOWS = (2, 4, 8, 16)
POOL_GROUP_W = 128
POOL_W = len(POOL_WINDOWS) * POOL_GROUP_W
POOL_HIST = max(POOL_WINDOWS) - 1
HIST_ROWS = 16
PAGE = 128
LANES = 128
NEG = float(jnp.finfo(jnp.float32).min)

ATT_BLOCK = 256
ATT_Q_ROWS = 512
DEC_PAGES_PER_STEP = 16
VMEM_LIMIT = 56 * 1024 * 1024


def _cparams(*sem):
    return pltpu.CompilerParams(dimension_semantics=sem, vmem_limit_bytes=VMEM_LIMIT)


def _rms(x, g):
    ms = jnp.mean(x * x, axis=-1, keepdims=True)
    return x * lax.rsqrt(ms + RMS_EPS) * g


def _log_sigmoid(x):
    return -(jnp.maximum(-x, 0.0) + jnp.log1p(jnp.exp(-jnp.abs(x))))


def _split3(x):
    a = x.astype(BF16).astype(F32)
    r = x - a
    b = r.astype(BF16).astype(F32)
    c = (r - b).astype(BF16).astype(F32)
    return a, b, c


def _dot(a, b):
    return jnp.dot(a, b, preferred_element_type=F32)


def _dot_nt(a, b):
    return lax.dot_general(a, b, (((1,), (1,)), ((), ())), preferred_element_type=F32)


def _ffn_body(x_ref, gpre_ref, gpost_ref, wig_ref, wiu_ref, wo_ref, o_ref, a_ref, *, chunk):
    x = x_ref[...]
    h = _rms(x, gpre_ref[...]).astype(BF16)
    d_ff = a_ref.shape[1]
    for c in range(d_ff // chunk):
        sl = slice(c * chunk, (c + 1) * chunk)
        g = _dot(h, wig_ref[:, sl])
        u = _dot(h, wiu_ref[:, sl])
        a_ref[:, sl] = (g * jax.nn.sigmoid(g) * u).astype(BF16)
    y = _dot(a_ref[...], wo_ref[...])
    o_ref[...] = x + 0.5 * _rms(y, gpost_ref[...])


def _ffn(x, g_pre, g_post, wi, wo, layer, tm):
    n, d = x.shape
    d_ff = wo.shape[1]
    const = lambda i: (layer, 0, 0)
    return pl.pallas_call(
        functools.partial(_ffn_body, chunk=256),
        grid=(n // tm,),
        in_specs=[
            pl.BlockSpec((tm, d), lambda i: (i, 0)),
            pl.BlockSpec((None, 1, d), const),
            pl.BlockSpec((None, 1, d), const),
            pl.BlockSpec((None, d, d_ff), const),
            pl.BlockSpec((None, d, d_ff), lambda i: (layer, 0, 1)),
            pl.BlockSpec((None, d_ff, d), const),
        ],
        out_specs=pl.BlockSpec((tm, d), lambda i: (i, 0)),
        out_shape=jax.ShapeDtypeStruct((n, d), F32),
        scratch_shapes=[pltpu.VMEM((tm, d_ff), BF16)],
        compiler_params=_cparams("arbitrary"),
        name="ffn_half_step",
    )(x, g_pre, g_post, wi, wi, wo)


def _inproj_rows_body(x_ref, g_ref, w_ref, bf_ref, u_ref, q_ref, sg_ref, *rest, with_kv):
    x = x_ref[...]
    h = _rms(x, g_ref[...]).astype(BF16)
    u_ref[...] = _dot(h, w_ref[:, 0:POOL_W])
    q_ref[...] = (_dot(h, w_ref[:, POOL_W:POOL_W + ATT_W]) * (HEAD_DIM ** -0.5)).astype(q_ref.dtype)
    g0 = POOL_W + ATT_W
    d = x.shape[1]
    for c in range(2 * d // 512):
        sg_ref[:, c * 512:(c + 1) * 512] = jax.nn.sigmoid(_dot(h, w_ref[:, g0 + c * 512:g0 + (c + 1) * 512]))
    if with_kv:
        k_ref, v_ref, lf_ref = rest
        k0 = g0 + 2 * d
        k_ref[...] = _dot(h, w_ref[:, k0:k0 + ATT_W])
        v_ref[...] = _dot(h, w_ref[:, k0 + ATT_W:k0 + 2 * ATT_W])
        zf = _dot(h, w_ref[:, k0 + 2 * ATT_W:k0 + 2 * ATT_W + LANES])
        lf_ref[...] = _log_sigmoid(zf + bf_ref[...])


def _inproj_rows(x, g, w, bf, layer, tm, with_kv):
    n, d = x.shape
    wcols = w.shape[2]
    const = lambda i: (layer, 0, 0)
    row = lambda i: (i, 0)
    out_shape = [jax.ShapeDtypeStruct((n, POOL_W), F32),
                 jax.ShapeDtypeStruct((n, ATT_W), BF16 if not with_kv else F32),
                 jax.ShapeDtypeStruct((n, 2 * d), F32)]
    out_specs = [pl.BlockSpec((tm, POOL_W), row), pl.BlockSpec((tm, ATT_W), row), pl.BlockSpec((tm, 2 * d), row)]
    if with_kv:
        out_shape += [jax.ShapeDtypeStruct((n, ATT_W), F32), jax.ShapeDtypeStruct((n, ATT_W), F32),
                      jax.ShapeDtypeStruct((n, LANES), F32)]
        out_specs += [pl.BlockSpec((tm, ATT_W), row), pl.BlockSpec((tm, ATT_W), row), pl.BlockSpec((tm, LANES), row)]
    return pl.pallas_call(
        functools.partial(_inproj_rows_body, with_kv=with_kv),
        grid=(n // tm,),
        in_specs=[
            pl.BlockSpec((tm, d), row),
            pl.BlockSpec((None, 1, d), const),
            pl.BlockSpec((None, d, wcols), const),
            pl.BlockSpec((None, 1, LANES), const),
        ],
        out_specs=out_specs,
        out_shape=out_shape,
        compiler_params=_cparams("arbitrary"),
        name="inproj_rows_kv" if with_kv else "inproj_rows",
    )(x, g, w, bf)


def _inproj_t_body(x_ref, g_ref, wt_ref, wft_ref, bft_ref, tri_ref, kt_ref, vt_ref, lft_ref, ct_ref,
                   carry_ref, *, seq_len):
    j = pl.program_id(1)
    tl = x_ref.shape[1]
    h = _rms(x_ref[0], g_ref[...]).astype(BF16)
    zt = _dot_nt(wt_ref[...], h)
    kt_ref[0] = zt[0:ATT_W]
    vt_ref[0] = zt[ATT_W:2 * ATT_W]
    zf = _dot_nt(wft_ref[...], h)[0:N_HEADS]
    lf = _log_sigmoid(zf + bft_ref[...])
    lft_ref[0] = lf

    @pl.when(j == 0)
    def _():
        carry_ref[...] = jnp.zeros_like(carry_ref)

    pos = j * tl + lax.broadcasted_iota(jnp.int32, lf.shape, 1)
    lf = jnp.where(pos < seq_len, lf, 0.0)
    a, b, c = _split3(lf)
    stack = jnp.concatenate([a, b, c, jnp.zeros_like(a)], axis=0).astype(BF16)
    cs = _dot(stack, tri_ref[...])
    run = cs[0:8] + cs[8:16] + cs[16:24] + carry_ref[...]
    ct_ref[0] = run
    carry_ref[...] = run[:, tl - 1:tl]


def _inproj_t(xp, g, wt, wft, bft, layer, tl):
    b, seq_len, d = xp.shape
    nt = pl.cdiv(seq_len, tl)
    tri = (lax.broadcasted_iota(jnp.int32, (tl, tl), 0) <= lax.broadcasted_iota(jnp.int32, (tl, tl), 1)).astype(BF16)
    const = lambda i, j: (layer, 0, 0)
    tile = lambda i, j: (i, 0, j)
    return pl.pallas_call(
        functools.partial(_inproj_t_body, seq_len=seq_len),
        grid=(b, nt),
        in_specs=[
            pl.BlockSpec((1, tl, d), lambda i, j: (i, j, 0)),
            pl.BlockSpec((None, 1, d), const),
            pl.BlockSpec((None, 2 * ATT_W, d), const),
            pl.BlockSpec((None, 16, d), const),
            pl.BlockSpec((None, N_HEADS, 1), const),
            pl.BlockSpec((tl, tl), lambda i, j: (0, 0)),
        ],
        out_specs=[
            pl.BlockSpec((1, ATT_W, tl), tile),
            pl.BlockSpec((1, ATT_W, tl), tile),
            pl.BlockSpec((1, N_HEADS, tl), tile),
            pl.BlockSpec((1, N_HEADS, tl), tile),
        ],
        out_shape=[
            jax.ShapeDtypeStruct((b, ATT_W, seq_len), F32),
            jax.ShapeDtypeStruct((b, ATT_W, seq_len), F32),
            jax.ShapeDtypeStruct((b, N_HEADS, seq_len), F32),
            jax.ShapeDtypeStruct((b, N_HEADS, seq_len), F32),
        ],
        scratch_shapes=[pltpu.VMEM((N_HEADS, 1), F32)],
        compiler_params=_cparams("arbitrary", "arbitrary"),
        name="inproj_feature_major",
    )(xp, g, wt, wft, bft, tri)


def _pool_mix_group(tot, cur, cnt, w_ref, scale_ref, g):
    cs = slice(g * POOL_GROUP_W, (g + 1) * POOL_GROUP_W)
    d = tot / cnt - cur
    return _dot(d.astype(BF16), w_ref[g]) * scale_ref[:, cs]


def _pool_seq_body(u_ref, w_ref, scale_ref, o_ref, pad_ref, *, chunk):
    seq_len = u_ref.shape[1]
    pad_ref[0:HIST_ROWS, :] = jnp.zeros((HIST_ROWS, POOL_W), F32)
    pad_ref[HIST_ROWS:HIST_ROWS + seq_len, :] = u_ref[0]
    for r0 in range(0, seq_len, chunk):
        pos = r0 + lax.broadcasted_iota(jnp.int32, (chunk, 1), 0)
        for g, w in enumerate(POOL_WINDOWS):
            cs = slice(g * POOL_GROUP_W, (g + 1) * POOL_GROUP_W)
            base = HIST_ROWS + r0
            cur = pad_ref[base:base + chunk, cs]
            tot = cur
            for i in range(1, w):
                tot = tot + pad_ref[base - i:base - i + chunk, cs]
            cnt = jnp.minimum(w, pos + 1).astype(F32)
            o_ref[0, r0:r0 + chunk, cs] = _pool_mix_group(tot, cur, cnt, w_ref, scale_ref, g).astype(o_ref.dtype)


def _pool_seq(u, w_grp, scale, layer):
    b, seq_len, _ = u.shape
    chunk = seq_len // 3
    const3 = lambda i: (layer, 0, 0)
    return pl.pallas_call(
        functools.partial(_pool_seq_body, chunk=chunk),
        grid=(b,),
        in_specs=[
            pl.BlockSpec((1, seq_len, POOL_W), lambda i: (i, 0, 0)),
            pl.BlockSpec((None, len(POOL_WINDOWS), POOL_GROUP_W, POOL_GROUP_W), lambda i: (layer, 0, 0, 0)),
            pl.BlockSpec((None, 1, POOL_W), const3),
        ],
        out_specs=pl.BlockSpec((1, seq_len, POOL_W), lambda i: (i, 0, 0)),
        out_shape=jax.ShapeDtypeStruct((b, seq_len, POOL_W), BF16),
        scratch_shapes=[pltpu.VMEM((HIST_ROWS + seq_len, POOL_W), F32)],
        compiler_params=_cparams("arbitrary"),
        name="pool_prompt",
    )(u, w_grp, scale)


def _pool_step_body(hist_ref, u_ref, w_ref, scale_ref, o_ref, *, pos0):
    n_hist = hist_ref.shape[1]
    t_new = u_ref.shape[0]

    def slab(i):
        return hist_ref[0, i] if i < n_hist else u_ref[i - n_hist]

    for t in range(t_new):
        end = n_hist + t
        for g, w in enumerate(POOL_WINDOWS):
            cs = slice(g * POOL_GROUP_W, (g + 1) * POOL_GROUP_W)
            cur = slab(end)[:, cs]
            tot = cur
            for i in range(1, w):
                if end - i >= 0:
                    tot = tot + slab(end - i)[:, cs]
            cnt = float(min(w, pos0 + t + 1))
            o_ref[t, :, cs] = _pool_mix_group(tot, cur, cnt, w_ref, scale_ref, g).astype(o_ref.dtype)


def _pool_step(hist, u_new, w_grp, scale, layer, pos0):
    _, n_hist, nb, _ = hist.shape
    t_new = u_new.shape[0]
    return pl.pallas_call(
        functools.partial(_pool_step_body, pos0=pos0),
        grid=(1,),
        in_specs=[
            pl.BlockSpec((1, n_hist, nb, POOL_W), lambda i: (layer, 0, 0, 0)),
            pl.BlockSpec((t_new, nb, POOL_W), lambda i: (0, 0, 0)),
            pl.BlockSpec((None, len(POOL_WINDOWS), POOL_GROUP_W, POOL_GROUP_W), lambda i: (layer, 0, 0, 0)),
            pl.BlockSpec((None, 1, POOL_W), lambda i: (layer, 0, 0)),
        ],
        out_specs=pl.BlockSpec((t_new, nb, POOL_W), lambda i: (0, 0, 0)),
        out_shape=jax.ShapeDtypeStruct((t_new, nb, POOL_W), F32),
        compiler_params=_cparams("arbitrary"),
        name="pool_sample",
    )(hist, u_new, w_grp, scale)


def _attn_body(q_ref, kt_ref, vt_ref, ct_ref, cc_ref, o_ref, qa_ref, ka_ref, va_ref, *, blk, q_rows):
    hp = pl.program_id(1)
    seq_len = q_ref.shape[1]
    nblk = ka_ref.shape[1]

    lane_q = lax.broadcasted_iota(jnp.int32, (seq_len, LANES), 1)
    row_k = lax.broadcasted_iota(jnp.int32, (LANES, seq_len), 0)
    lane_h = lax.broadcasted_iota(jnp.int32, (seq_len, N_HEADS), 1)
    qpair = q_ref[0].astype(F32)
    kpair = kt_ref[0]
    n_full = seq_len // blk
    tail = seq_len - n_full * blk
    assert nblk == n_full + 1 and tail % 16 == 0

    def fill_cols(dst, val):
        for j in range(n_full):
            dst[j] = val[:, j * blk:(j + 1) * blk]
        dst[n_full] = jnp.zeros(dst.shape[1:], dst.dtype)
        dst[n_full, :, 0:tail] = val[:, n_full * blk:seq_len]

    fill_cols(va_ref, vt_ref[0].astype(BF16))

    for hh in range(2):
        head = 2 * hp + hh
        f0 = HEAD_DIM * (1 - hh)
        ccol = jnp.sum(jnp.where(lane_h == head, cc_ref[0], 0.0), axis=1, keepdims=True)
        crow = ct_ref[0, pl.ds(head, 1), :]
        q1, q2, q3 = _split3(ccol)
        k1, k2, k3 = _split3(crow)
        in_head_q = (lane_q >= HEAD_DIM * hh) & (lane_q < HEAD_DIM * (hh + 1))
        qa = jnp.where(in_head_q, qpair,
             jnp.where(lane_q == f0, q1, jnp.where(lane_q == f0 + 1, q2, jnp.where(lane_q == f0 + 2, q3,
             jnp.where((lane_q >= f0 + 3) & (lane_q < f0 + 6), 1.0, 0.0)))))
        in_head_k = (row_k >= HEAD_DIM * hh) & (row_k < HEAD_DIM * (hh + 1))
        ka = jnp.where(in_head_k, kpair,
             jnp.where((row_k >= f0) & (row_k < f0 + 3), 1.0,
             jnp.where(row_k == f0 + 3, -k1, jnp.where(row_k == f0 + 4, -k2, jnp.where(row_k == f0 + 5, -k3, 0.0)))))
        qa_ref[hh] = qa.astype(BF16)
        fill_cols(ka_ref.at[hh], ka.astype(BF16))

    def update(state, s, j):
        m, l, acc = state
        m_new = jnp.maximum(m, jnp.max(s, axis=1, keepdims=True))
        alpha = jnp.exp(m - m_new)
        p = jnp.exp(s - m_new)
        l = alpha * l + jnp.sum(p, axis=1, keepdims=True)
        acc = alpha * acc + _dot_nt(p.astype(BF16), va_ref[j])
        return m_new, l, acc

    n_q = max(seq_len // q_rows, 1)
    for i in range(n_q):
        r0 = i * q_rows
        r1 = seq_len if i == n_q - 1 else r0 + q_rows
        n = r1 - r0
        qb = [qa_ref[hh, r0:r1, :] for hh in range(2)]
        n_open = r0 // blk
        n_need = pl.cdiv(r1, blk)

        def open_block(j, states, qb=qb):
            return tuple(update(states[hh], _dot(qb[hh], ka_ref[hh, j]), j) for hh in range(2))

        init = (jnp.full((n, 1), NEG, F32), jnp.zeros((n, 1), F32), jnp.zeros((n, LANES), F32))
        states = lax.fori_loop(0, n_open, open_block, (init, init))
        row = r0 + lax.broadcasted_iota(jnp.int32, (n, blk), 0)
        col = lax.broadcasted_iota(jnp.int32, (n, blk), 1)
        for j in range(n_open, n_need):
            visible = (j * blk + col) <= row
            states = tuple(update(states[hh], jnp.where(visible, _dot(qb[hh], ka_ref[hh, j]), NEG), j)
                           for hh in range(2))
        (_, l0, acc0), (_, l1, acc1) = states
        lane_o = lax.broadcasted_iota(jnp.int32, (n, LANES), 1)
        o_ref[0, r0:r1, :] = jnp.where(lane_o < HEAD_DIM, acc0 / l0, acc1 / l1).astype(o_ref.dtype)


def _attn_prompt(q, kt, vt, ct, cc):
    b, seq_len, _ = q.shape
    blk = ATT_BLOCK
    nblk = pl.cdiv(seq_len, blk)
    return pl.pallas_call(
        functools.partial(_attn_body, blk=blk, q_rows=ATT_Q_ROWS),
        grid=(b, N_HEADS // 2),
        in_specs=[
            pl.BlockSpec((1, seq_len, LANES), lambda i, p: (i, 0, p)),
            pl.BlockSpec((1, LANES, seq_len), lambda i, p: (i, p, 0)),
            pl.BlockSpec((1, LANES, seq_len), lambda i, p: (i, p, 0)),
            pl.BlockSpec((1, N_HEADS, seq_len), lambda i, p: (i, 0, 0)),
            pl.BlockSpec((1, seq_len, N_HEADS), lambda i, p: (i, 0, 0)),
        ],
        out_specs=pl.BlockSpec((1, seq_len, LANES), lambda i, p: (i, 0, p)),
        out_shape=jax.ShapeDtypeStruct((b, seq_len, ATT_W), BF16),
        scratch_shapes=[
            pltpu.VMEM((2, seq_len, LANES), BF16),
            pltpu.VMEM((2, nblk, LANES, blk), BF16),
            pltpu.VMEM((nblk, LANES, blk), BF16),
        ],
        compiler_params=_cparams("arbitrary", "arbitrary"),
        name="attn_prompt",
    )(q, kt, vt, ct, cc)


def _decode_body(pt_ref, q_ref, kn_ref, vn_ref, lfn_ref, *rest, n_pg, t_new):
    k_refs = rest[0:n_pg]
    v_refs = rest[n_pg:2 * n_pg]
    f_refs = rest[2 * n_pg:3 * n_pg]
    o_ref = rest[3 * n_pg]
    qbd_ref, m_ref, l_ref, acc_ref, carry_ref, cnew_ref = rest[3 * n_pg + 1:]
    j = pl.program_id(1)
    n_rows = t_new * N_HEADS

    row_h = lax.broadcasted_iota(jnp.int32, (n_rows, ATT_W), 0) & (N_HEADS - 1)
    own = (lax.broadcasted_iota(jnp.int32, (n_rows, ATT_W), 1) >> HEAD_SHIFT) == row_h

    r_i = lax.broadcasted_iota(jnp.int32, (n_rows, n_rows), 0)
    c_i = lax.broadcasted_iota(jnp.int32, (n_rows, n_rows), 1)
    same_head = (r_i & (N_HEADS - 1)) == (c_i & (N_HEADS - 1))

    def new_token_sum(lo):
        sel = same_head & ((c_i >> TOK_SHIFT) > lo) & ((c_i >> TOK_SHIFT) <= (r_i >> TOK_SHIFT))
        return jnp.sum(jnp.where(sel, lfn_ref[0], 0.0), axis=1, keepdims=True)

    @pl.when(j == 0)
    def _():
        q4 = q_ref[0]
        rep = jnp.concatenate([jnp.broadcast_to(q4[t:t + 1], (N_HEADS, ATT_W)) for t in range(t_new)], axis=0)
        qbd_ref[...] = jnp.where(own, rep, 0.0).astype(BF16)
        m_ref[...] = jnp.full_like(m_ref, NEG)
        l_ref[...] = jnp.zeros_like(l_ref)
        acc_ref[...] = jnp.zeros_like(acc_ref)
        carry_ref[...] = jnp.zeros_like(carry_ref)
        cnew_ref[...] = new_token_sum(-1)

    def update(s, pv):
        m = m_ref[...]
        m_new = jnp.maximum(m, jnp.max(s, axis=1, keepdims=True))
        alpha = jnp.exp(m - m_new)
        p = jnp.exp(s - m_new)
        l_ref[...] = alpha * l_ref[...] + jnp.sum(p, axis=1, keepdims=True)
        p = p.astype(BF16)
        out = pv(p[:, 0:PAGE], 0)
        for k in range(1, s.shape[1] // PAGE):
            out = out + pv(p[:, k * PAGE:(k + 1) * PAGE], k)
        acc_ref[...] = alpha * acc_ref[...] + out
        m_ref[...] = m_new

    later = (lax.broadcasted_iota(jnp.int32, (PAGE, PAGE), 0) >
             lax.broadcasted_iota(jnp.int32, (PAGE, PAGE), 1)).astype(BF16)
    qbd = qbd_ref[...]
    f_all = jnp.concatenate([f_refs[p][...] for p in range(n_pg)], axis=0)
    a, b, c = _split3(f_all)
    cs = _dot(jnp.concatenate([a, b, c], axis=0).astype(BF16), later)
    nf = n_pg * N_HEADS
    within = cs[0:nf] + cs[nf:2 * nf] + cs[2 * nf:3 * nf]
    total = jnp.sum(f_all, axis=1, keepdims=True)
    carry = carry_ref[...]
    tiles = []
    for p in range(n_pg):
        rows = slice(p * N_HEADS, (p + 1) * N_HEADS)
        suffix = within[rows] + carry
        carry = carry + total[rows]
        kt = k_refs[p][...].reshape(ATT_W, PAGE).astype(BF16)
        tiles.append(_dot(qbd, kt) + jnp.concatenate([suffix] * t_new, axis=0))
    carry_ref[...] = carry
    s = jnp.concatenate(tiles, axis=1) + cnew_ref[...]
    update(s, lambda pb, p: _dot_nt(pb, v_refs[p][...].reshape(ATT_W, PAGE).astype(BF16)))

    @pl.when(j == pl.num_programs(1) - 1)
    def _():
        zpad = jnp.zeros((PAGE - t_new, ATT_W), F32)
        kpad = jnp.concatenate([kn_ref[0], zpad], axis=0).astype(BF16)
        vpad = jnp.concatenate([vn_ref[0], zpad], axis=0).astype(BF16)
        lane = lax.broadcasted_iota(jnp.int32, (n_rows, PAGE), 1)
        tok = lax.broadcasted_iota(jnp.int32, (n_rows, PAGE), 0) >> TOK_SHIFT
        bias = jnp.zeros((n_rows, PAGE), F32)
        for t in range(t_new):
            bias = jnp.where(lane == t, new_token_sum(t), bias)
        s = jnp.where(lane <= tok, _dot_nt(qbd, kpad) + bias, NEG)
        update(s, lambda pb, k: _dot(pb, vpad))
        out = jnp.where(own, acc_ref[...] / l_ref[...], 0.0)
        o_ref[0] = jnp.sum(out.reshape(t_new, N_HEADS, ATT_W), axis=1)


def _decode(page_table, q, k_new, v_new, lf_new, ckt, cvt, cft, layer):
    nb, t_new, _ = q.shape
    n_pages = page_table.shape[1]
    n_pg = DEC_PAGES_PER_STEP
    n_rows = t_new * N_HEADS

    def page_spec(p, shape):
        zeros = (0,) * len(shape)

        def index(b, j, pt):
            return (layer, pt[b, n_pages - 1 - (j * n_pg + p)]) + zeros

        return pl.BlockSpec((None, None) + shape, index)

    new = lambda b, j, pt: (b, 0, 0)
    in_specs = [pl.BlockSpec((1, t_new, ATT_W), new), pl.BlockSpec((1, t_new, ATT_W), new),
                pl.BlockSpec((1, t_new, ATT_W), new), pl.BlockSpec((1, 1, n_rows), new)]
    in_specs += [page_spec(p, (N_HEADS, HEAD_DIM, PAGE)) for p in range(n_pg)]
    in_specs += [page_spec(p, (N_HEADS, HEAD_DIM, PAGE)) for p in range(n_pg)]
    in_specs += [page_spec(p, (N_HEADS, PAGE)) for p in range(n_pg)]
    grid_spec = pltpu.PrefetchScalarGridSpec(
        num_scalar_prefetch=1,
        grid=(nb, n_pages // n_pg),
        in_specs=in_specs,
        out_specs=pl.BlockSpec((1, t_new, ATT_W), new),
        scratch_shapes=[
            pltpu.VMEM((n_rows, ATT_W), BF16),
            pltpu.VMEM((n_rows, 1), F32),
            pltpu.VMEM((n_rows, 1), F32),
            pltpu.VMEM((n_rows, ATT_W), F32),
            pltpu.VMEM((N_HEADS, 1), F32),
            pltpu.VMEM((n_rows, 1), F32),
        ],
    )
    return pl.pallas_call(
        functools.partial(_decode_body, n_pg=n_pg, t_new=t_new),
        grid_spec=grid_spec,
        out_shape=jax.ShapeDtypeStruct((nb, t_new, ATT_W), F32),
        compiler_params=_cparams("arbitrary", "arbitrary"),
        name="attn_decode",
    )(page_table, q, k_new, v_new, lf_new, *([ckt] * n_pg), *([cvt] * n_pg), *([cft] * n_pg))


def _merge_body(x_ref, pu_ref, ao_ref, sg_ref, wpu_ref, wau_ref, wout_ref, gpost_ref, o_ref):
    d = x_ref.shape[1]
    m = (sg_ref[:, 0:d] * _dot(pu_ref[...].astype(BF16), wpu_ref[...])
         + sg_ref[:, d:2 * d] * _dot(ao_ref[...].astype(BF16), wau_ref[...]))
    y = _dot(m.astype(BF16), wout_ref[...])
    o_ref[...] = x_ref[...] + _rms(y, gpost_ref[...])


def _merge(x, pu, ao, sg, wpu, wau, wout, g_post, layer, tm):
    n, d = x.shape
    row = lambda i: (i, 0)
    const = lambda i: (layer, 0, 0)
    return pl.pallas_call(
        _merge_body,
        grid=(n // tm,),
        in_specs=[
            pl.BlockSpec((tm, d), row),
            pl.BlockSpec((tm, POOL_W), row),
            pl.BlockSpec((tm, ATT_W), row),
            pl.BlockSpec((tm, 2 * d), row),
            pl.BlockSpec((None, POOL_W, d), const),
            pl.BlockSpec((None, ATT_W, d), const),
            pl.BlockSpec((None, d, d), const),
            pl.BlockSpec((None, 1, d), const),
        ],
        out_specs=pl.BlockSpec((tm, d), row),
        out_shape=jax.ShapeDtypeStruct((n, d), F32),
        compiler_params=_cparams("arbitrary"),
        name="merge",
    )(x, pu, ao, sg, wpu, wau, wout, g_post)


def _row_tile(n, cap):
    best = None
    for t in range(16, cap + 1, 16):
        if n % t == 0:
            best = t
    assert best is not None, n
    return best


def kernel(x_prompt, x_sample, cache_k, cache_v, cache_logf, state_pool, page_table, meta_tokens,
           ln_ffn1_pre, ln_ffn1_post, ffn1_wi, ffn1_wo, ln_mix_pre, ln_mix_post, w_in, b_forget,
           pool_w_grp, pool_scale, w_pool_up, w_att_up, w_out, ln_ffn2_pre, ln_ffn2_post,
           ffn2_wi, ffn2_wo):
    nbp, seq, d = x_prompt.shape
    nbs, t_new, _ = x_sample.shape
    depth = w_in.shape[0]
    n_pages = page_table.shape[1]
    past_len = n_pages * PAGE
    seq_len = seq + N_META

    meta = jnp.broadcast_to(meta_tokens.astype(x_prompt.dtype)[None], (nbp, N_META, d))
    xp = jnp.concatenate([meta, x_prompt], axis=1).reshape(nbp * seq_len, d)
    xs = x_sample.reshape(nbs * t_new, d)
    tm_p = _row_tile(nbp * seq_len, 768)
    tm_s = nbs * t_new

    bf = lambda w: w.astype(BF16)
    vec = lambda g: g[:, None, :]
    o1, o2, o3, o4, o5 = POOL_W, POOL_W + ATT_W, POOL_W + 2 * ATT_W, POOL_W + 3 * ATT_W, POOL_W + 3 * ATT_W + N_HEADS
    w_in_b = bf(w_in)
    w_f_pad = jnp.pad(w_in_b[:, :, o4:o5], ((0, 0), (0, 0), (0, LANES - N_HEADS)))
    w_rows_p = jnp.concatenate([w_in_b[:, :, 0:o2], w_in_b[:, :, o5:]], axis=2)
    w_rows_s = jnp.concatenate([w_rows_p, w_in_b[:, :, o2:o4], w_f_pad], axis=2)
    w_kv_t = jnp.transpose(w_in_b[:, :, o2:o4], (0, 2, 1))
    w_f_t = jnp.pad(jnp.transpose(w_in_b[:, :, o4:o5], (0, 2, 1)), ((0, 0), (0, 16 - N_HEADS), (0, 0)))
    bf_row = jnp.pad(b_forget, ((0, 0), (0, LANES - N_HEADS)))[:, None, :]
    bf_col = b_forget[:, :, None]
    ffn1_wi_b, ffn1_wo_b, ffn2_wi_b, ffn2_wo_b = bf(ffn1_wi), bf(ffn1_wo), bf(ffn2_wi), bf(ffn2_wo)
    w_grp_b, wpu_b, wau_b, wout_b = bf(pool_w_grp), bf(w_pool_up), bf(w_att_up), bf(w_out)
    scale3 = vec(pool_scale)
    g1pre, g1post, gmpre, gmpost, g2pre, g2post = (vec(g) for g in (
        ln_ffn1_pre, ln_ffn1_post, ln_mix_pre, ln_mix_post, ln_ffn2_pre, ln_ffn2_post))

    ckt = jnp.transpose(cache_k, (0, 1, 3, 4, 2))
    cvt = jnp.transpose(cache_v, (0, 1, 3, 4, 2))
    cft = jnp.transpose(cache_logf, (0, 1, 3, 2))
    hist = jnp.transpose(state_pool, (0, 2, 1, 3))

    kp_l, vp_l, fp_l, pp_l, ks_l, vs_l, fs_l, ps_l = [], [], [], [], [], [], [], []
    for l in range(depth):
        xp = _ffn(xp, g1pre, g1post, ffn1_wi_b, ffn1_wo_b, l, tm_p)
        xs = _ffn(xs, g1pre, g1post, ffn1_wi_b, ffn1_wo_b, l, tm_s)

        u, q, sg = _inproj_rows(xp, gmpre, w_rows_p, bf_row, l, tm_p, with_kv=False)
        kt, vt, lft, ct = _inproj_t(xp.reshape(nbp, seq_len, d), gmpre, w_kv_t, w_f_t, bf_col, l, 256)
        u3 = u.reshape(nbp, seq_len, POOL_W)
        pu = _pool_seq(u3, w_grp_b, scale3, l)
        ao = _attn_prompt(q.reshape(nbp, seq_len, ATT_W), kt, vt, ct, jnp.transpose(ct, (0, 2, 1)))
        xp = _merge(xp, pu.reshape(-1, POOL_W), ao.reshape(-1, ATT_W), sg, wpu_b, wau_b, wout_b, gmpost, l, tm_p)
        kp_l.append(kt); vp_l.append(vt); fp_l.append(lft); pp_l.append(u3[:, seq_len - POOL_HIST:])

        u, q, sg, k, v, lf = _inproj_rows(xs, gmpre, w_rows_s, bf_row, l, tm_s, with_kv=True)
        u_tb = jnp.transpose(u.reshape(nbs, t_new, POOL_W), (1, 0, 2))
        pu = jnp.transpose(_pool_step(hist, u_tb, w_grp_b, scale3, l, past_len), (1, 0, 2))
        lf8 = lf[:, 0:N_HEADS].reshape(nbs, t_new, N_HEADS)
        ao = _decode(page_table, q.reshape(nbs, t_new, ATT_W), k.reshape(nbs, t_new, ATT_W),
                     v.reshape(nbs, t_new, ATT_W), lf8.reshape(nbs, 1, t_new * N_HEADS), ckt, cvt, cft, l)
        xs = _merge(xs, pu.reshape(-1, POOL_W), ao.reshape(-1, ATT_W), sg, wpu_b, wau_b, wout_b, gmpost, l, tm_s)
        ks_l.append(k.reshape(nbs, t_new, N_HEADS, HEAD_DIM)); vs_l.append(v.reshape(nbs, t_new, N_HEADS, HEAD_DIM))
        fs_l.append(lf8)
        ps_l.append(jnp.concatenate([state_pool[l], u.reshape(nbs, t_new, POOL_W)], axis=1)[:, t_new:])

        xp = _ffn(xp, g2pre, g2post, ffn2_wi_b, ffn2_wo_b, l, tm_p)
        xs = _ffn(xs, g2pre, g2post, ffn2_wi_b, ffn2_wo_b, l, tm_s)

    y_prompt = xp.reshape(nbp, seq_len, d)[:, N_META:]
    y_sample = xs.reshape(nbs, t_new, d)
    heads_t = lambda a: jnp.transpose(jnp.stack(a).reshape(depth, nbp, N_HEADS, HEAD_DIM, seq_len), (0, 1, 4, 2, 3))
    return (y_prompt, y_sample,
            heads_t(kp_l), heads_t(vp_l), jnp.transpose(jnp.stack(fp_l), (0, 1, 3, 2)), jnp.stack(pp_l),
            jnp.stack(ks_l), jnp.stack(vs_l), jnp.stack(fs_l), jnp.stack(ps_l))
```

```python
import functools

import jax
import jax.numpy as jnp
from jax import lax
from jax.experimental import pallas as pl
from jax.experimental.pallas import tpu as pltpu

F32 = jnp.float32
BF16 = jnp.bfloat16

RMS_EPS = 1e-6
N_META = 16
N_HEADS = 8
HEAD_DIM = 64
HEAD_SHIFT = 6
TOK_SHIFT = 3
ATT_W = N_HEADS * HEAD_DIM
POOL_WINDOWS = (2, 4, 8, 16)
POOL_GROUP_W = 128
POOL_W = len(POOL_WINDOWS) * POOL_GROUP_W
POOL_HIST = max(POOL_WINDOWS) - 1
HIST_ROWS = 16
PAGE = 128
LANES = 128
NEG = float(jnp.finfo(jnp.float32).min)

ATT_BLOCK = 256
ATT_Q_ROWS = 512
KV_TILE = 512
DEC_PAGES_PER_STEP = 16
VMEM_LIMIT = 56 * 1024 * 1024


def _cparams(*sem):
    return pltpu.CompilerParams(dimension_semantics=sem, vmem_limit_bytes=VMEM_LIMIT)


def _rms(x, g):
    ms = jnp.mean(x * x, axis=-1, keepdims=True)
    return x * lax.rsqrt(ms + RMS_EPS) * g


def _log_sigmoid(x):
    return -(jnp.maximum(-x, 0.0) + jnp.log1p(jnp.exp(-jnp.abs(x))))


def _split3(x):
    a = x.astype(BF16).astype(F32)
    r = x - a
    b = r.astype(BF16).astype(F32)
    c = (r - b).astype(BF16).astype(F32)
    return a, b, c


def _dot(a, b):
    return jnp.dot(a, b, preferred_element_type=F32)


def _dot_nt(a, b):
    return lax.dot_general(a, b, (((1,), (1,)), ((), ())), preferred_element_type=F32)


def _ffn_body(x_ref, gpre_ref, gpost_ref, wig_ref, wiu_ref, wo_ref, o_ref, a_ref, *, chunk):
    x = x_ref[...]
    h = _rms(x, gpre_ref[...]).astype(BF16)
    d_ff = a_ref.shape[1]
    for c in range(d_ff // chunk):
        sl = slice(c * chunk, (c + 1) * chunk)
        g = _dot(h, wig_ref[:, sl])
        u = _dot(h, wiu_ref[:, sl])
        a_ref[:, sl] = (g * jax.nn.sigmoid(g) * u).astype(BF16)
    y = _dot(a_ref[...], wo_ref[...])
    o_ref[...] = x + 0.5 * _rms(y, gpost_ref[...])


def _ffn(x, g_pre, g_post, wi, wo, layer, tm):
    n, d = x.shape
    d_ff = wo.shape[1]
    const = lambda i: (layer, 0, 0)
    return pl.pallas_call(
        functools.partial(_ffn_body, chunk=256),
        grid=(n // tm,),
        in_specs=[
            pl.BlockSpec((tm, d), lambda i: (i, 0)),
            pl.BlockSpec((None, 1, d), const),
            pl.BlockSpec((None, 1, d), const),
            pl.BlockSpec((None, d, d_ff), const),
            pl.BlockSpec((None, d, d_ff), lambda i: (layer, 0, 1)),
            pl.BlockSpec((None, d_ff, d), const),
        ],
        out_specs=pl.BlockSpec((tm, d), lambda i: (i, 0)),
        out_shape=jax.ShapeDtypeStruct((n, d), F32),
        scratch_shapes=[pltpu.VMEM((tm, d_ff), BF16)],
        compiler_params=_cparams("arbitrary"),
        name="ffn_half_step",
    )(x, g_pre, g_post, wi, wi, wo)


def _inproj_rows_body(x_ref, g_ref, w_ref, bf_ref, u_ref, q_ref, *rest, with_kv):
    x = x_ref[...]
    h = _rms(x, g_ref[...]).astype(BF16)
    u_ref[...] = _dot(h, w_ref[:, 0:POOL_W])
    q_ref[...] = (_dot(h, w_ref[:, POOL_W:POOL_W + ATT_W]) * (HEAD_DIM ** -0.5)).astype(q_ref.dtype)
    if with_kv:
        k_ref, v_ref, lf_ref = rest
        k0 = POOL_W + ATT_W
        k_ref[...] = _dot(h, w_ref[:, k0:k0 + ATT_W])
        v_ref[...] = _dot(h, w_ref[:, k0 + ATT_W:k0 + 2 * ATT_W])
        zf = _dot(h, w_ref[:, k0 + 2 * ATT_W:k0 + 2 * ATT_W + LANES])
        lf_ref[...] = _log_sigmoid(zf + bf_ref[...])


def _inproj_rows(x, g, w, bf, layer, tm, with_kv):
    n, d = x.shape
    wcols = w.shape[2]
    const = lambda i: (layer, 0, 0)
    row = lambda i: (i, 0)
    out_shape = [jax.ShapeDtypeStruct((n, POOL_W), F32),
                 jax.ShapeDtypeStruct((n, ATT_W), BF16 if not with_kv else F32)]
    out_specs = [pl.BlockSpec((tm, POOL_W), row), pl.BlockSpec((tm, ATT_W), row)]
    if with_kv:
        out_shape += [jax.ShapeDtypeStruct((n, ATT_W), F32), jax.ShapeDtypeStruct((n, ATT_W), F32),
                      jax.ShapeDtypeStruct((n, LANES), F32)]
        out_specs += [pl.BlockSpec((tm, ATT_W), row), pl.BlockSpec((tm, ATT_W), row), pl.BlockSpec((tm, LANES), row)]
    return pl.pallas_call(
        functools.partial(_inproj_rows_body, with_kv=with_kv),
        grid=(n // tm,),
        in_specs=[
            pl.BlockSpec((tm, d), row),
            pl.BlockSpec((None, 1, d), const),
            pl.BlockSpec((None, d, wcols), const),
            pl.BlockSpec((None, 1, LANES), const),
        ],
        out_specs=out_specs,
        out_shape=out_shape,
        compiler_params=_cparams("arbitrary"),
        name="inproj_rows_kv" if with_kv else "inproj_rows",
    )(x, g, w, bf)


def _inproj_t_body(x_ref, g_ref, wt_ref, wft_ref, bft_ref, tri_ref, kt_ref, vt_ref, lft_ref, ct_ref,
                   carry_ref, *, seq_len):
    j = pl.program_id(1)
    tl = x_ref.shape[1]
    h = _rms(x_ref[0], g_ref[...]).astype(BF16)
    zt = _dot_nt(wt_ref[...], h)
    kt_ref[0] = zt[0:ATT_W]
    vt_ref[0] = zt[ATT_W:2 * ATT_W]
    zf = _dot_nt(wft_ref[...], h)[0:N_HEADS]
    lf = _log_sigmoid(zf + bft_ref[...])
    lft_ref[0] = lf

    @pl.when(j == 0)
    def _():
        carry_ref[...] = jnp.zeros_like(carry_ref)

    pos = j * tl + lax.broadcasted_iota(jnp.int32, lf.shape, 1)
    lf = jnp.where(pos < seq_len, lf, 0.0)
    a, b, c = _split3(lf)
    stack = jnp.concatenate([a, b, c, jnp.zeros_like(a)], axis=0).astype(BF16)
    cs = _dot(stack, tri_ref[...])
    run = cs[0:8] + cs[8:16] + cs[16:24] + carry_ref[...]
    ct_ref[0] = run
    carry_ref[...] = run[:, tl - 1:tl]


def _inproj_t(xp, g, wt, wft, bft, layer, tl):
    b, seq_len, d = xp.shape
    nt = pl.cdiv(seq_len, tl)
    tri = (lax.broadcasted_iota(jnp.int32, (tl, tl), 0) <= lax.broadcasted_iota(jnp.int32, (tl, tl), 1)).astype(BF16)
    const = lambda i, j: (layer, 0, 0)
    tile = lambda i, j: (i, 0, j)
    return pl.pallas_call(
        functools.partial(_inproj_t_body, seq_len=seq_len),
        grid=(b, nt),
        in_specs=[
            pl.BlockSpec((1, tl, d), lambda i, j: (i, j, 0)),
            pl.BlockSpec((None, 1, d), const),
            pl.BlockSpec((None, 2 * ATT_W, d), const),
            pl.BlockSpec((None, 16, d), const),
            pl.BlockSpec((None, N_HEADS, 1), const),
            pl.BlockSpec((tl, tl), lambda i, j: (0, 0)),
        ],
        out_specs=[
            pl.BlockSpec((1, ATT_W, tl), tile),
            pl.BlockSpec((1, ATT_W, tl), tile),
            pl.BlockSpec((1, N_HEADS, tl), tile),
            pl.BlockSpec((1, N_HEADS, tl), tile),
        ],
        out_shape=[
            jax.ShapeDtypeStruct((b, ATT_W, seq_len), F32),
            jax.ShapeDtypeStruct((b, ATT_W, seq_len), F32),
            jax.ShapeDtypeStruct((b, N_HEADS, seq_len), F32),
            jax.ShapeDtypeStruct((b, N_HEADS, seq_len), F32),
        ],
        scratch_shapes=[pltpu.VMEM((N_HEADS, 1), F32)],
        compiler_params=_cparams("arbitrary", "arbitrary"),
        name="inproj_feature_major",
    )(xp, g, wt, wft, bft, tri)


def _pool_mix_group(tot, cur, cnt, w_ref, scale_ref, g):
    cs = slice(g * POOL_GROUP_W, (g + 1) * POOL_GROUP_W)
    d = tot / cnt - cur
    return _dot(d.astype(BF16), w_ref[g]) * scale_ref[:, cs]


def _pool_seq_body(u_ref, w_ref, scale_ref, o_ref, pad_ref, *, chunk):
    seq_len = u_ref.shape[1]
    pad_ref[0:HIST_ROWS, :] = jnp.zeros((HIST_ROWS, POOL_W), F32)
    pad_ref[HIST_ROWS:HIST_ROWS + seq_len, :] = u_ref[0]
    for r0 in range(0, seq_len, chunk):
        pos = r0 + lax.broadcasted_iota(jnp.int32, (chunk, 1), 0)
        for g, w in enumerate(POOL_WINDOWS):
            cs = slice(g * POOL_GROUP_W, (g + 1) * POOL_GROUP_W)
            base = HIST_ROWS + r0
            cur = pad_ref[base:base + chunk, cs]
            tot = cur
            for i in range(1, w):
                tot = tot + pad_ref[base - i:base - i + chunk, cs]
            cnt = jnp.minimum(w, pos + 1).astype(F32)
            o_ref[0, r0:r0 + chunk, cs] = _pool_mix_group(tot, cur, cnt, w_ref, scale_ref, g).astype(o_ref.dtype)


def _pool_seq(u, w_grp, scale, layer):
    b, seq_len, _ = u.shape
    chunk = seq_len // 3
    const3 = lambda i: (layer, 0, 0)
    return pl.pallas_call(
        functools.partial(_pool_seq_body, chunk=chunk),
        grid=(b,),
        in_specs=[
            pl.BlockSpec((1, seq_len, POOL_W), lambda i: (i, 0, 0)),
            pl.BlockSpec((None, len(POOL_WINDOWS), POOL_GROUP_W, POOL_GROUP_W), lambda i: (layer, 0, 0, 0)),
            pl.BlockSpec((None, 1, POOL_W), const3),
        ],
        out_specs=pl.BlockSpec((1, seq_len, POOL_W), lambda i: (i, 0, 0)),
        out_shape=jax.ShapeDtypeStruct((b, seq_len, POOL_W), BF16),
        scratch_shapes=[pltpu.VMEM((HIST_ROWS + seq_len, POOL_W), F32)],
        compiler_params=_cparams("arbitrary"),
        name="pool_prompt",
    )(u, w_grp, scale)


def _pool_step_body(hist_ref, u_ref, w_ref, scale_ref, o_ref, *, pos0):
    n_hist = hist_ref.shape[1]
    t_new = u_ref.shape[0]

    def slab(i):
        return hist_ref[0, i] if i < n_hist else u_ref[i - n_hist]

    for t in range(t_new):
        end = n_hist + t
        for g, w in enumerate(POOL_WINDOWS):
            cs = slice(g * POOL_GROUP_W, (g + 1) * POOL_GROUP_W)
            cur = slab(end)[:, cs]
            tot = cur
            for i in range(1, w):
                if end - i >= 0:
                    tot = tot + slab(end - i)[:, cs]
            cnt = float(min(w, pos0 + t + 1))
            o_ref[t, :, cs] = _pool_mix_group(tot, cur, cnt, w_ref, scale_ref, g).astype(o_ref.dtype)


def _pool_step(hist, u_new, w_grp, scale, layer, pos0):
    _, n_hist, nb, _ = hist.shape
    t_new = u_new.shape[0]
    return pl.pallas_call(
        functools.partial(_pool_step_body, pos0=pos0),
        grid=(1,),
        in_specs=[
            pl.BlockSpec((1, n_hist, nb, POOL_W), lambda i: (layer, 0, 0, 0)),
            pl.BlockSpec((t_new, nb, POOL_W), lambda i: (0, 0, 0)),
            pl.BlockSpec((None, len(POOL_WINDOWS), POOL_GROUP_W, POOL_GROUP_W), lambda i: (layer, 0, 0, 0)),
            pl.BlockSpec((None, 1, POOL_W), lambda i: (layer, 0, 0)),
        ],
        out_specs=pl.BlockSpec((t_new, nb, POOL_W), lambda i: (0, 0, 0)),
        out_shape=jax.ShapeDtypeStruct((t_new, nb, POOL_W), F32),
        compiler_params=_cparams("arbitrary"),
        name="pool_sample",
    )(hist, u_new, w_grp, scale)


def _attn_body(q_ref, kt_ref, vt_ref, ct_ref, cc_ref, o_ref, qa_ref, ka_ref, va_ref, *, blk, q_rows):
    hp = pl.program_id(1)
    seq_len = q_ref.shape[1]
    nblk = ka_ref.shape[1]

    lane_q = lax.broadcasted_iota(jnp.int32, (seq_len, LANES), 1)
    row_k = lax.broadcasted_iota(jnp.int32, (LANES, seq_len), 0)
    lane_h = lax.broadcasted_iota(jnp.int32, (seq_len, N_HEADS), 1)
    qpair = q_ref[0].astype(F32)
    kpair = kt_ref[0]
    n_full = seq_len // blk
    tail = seq_len - n_full * blk
    assert nblk == n_full + 1 and tail % 16 == 0

    def fill_cols(dst, val):
        for j in range(n_full):
            dst[j] = val[:, j * blk:(j + 1) * blk]
        dst[n_full] = jnp.zeros(dst.shape[1:], dst.dtype)
        dst[n_full, :, 0:tail] = val[:, n_full * blk:seq_len]

    fill_cols(va_ref, vt_ref[0].astype(BF16))

    for hh in range(2):
        head = 2 * hp + hh
        f0 = HEAD_DIM * (1 - hh)
        ccol = jnp.sum(jnp.where(lane_h == head, cc_ref[0], 0.0), axis=1, keepdims=True)
        crow = ct_ref[0, pl.ds(head, 1), :]
        q1, q2, q3 = _split3(ccol)
        k1, k2, k3 = _split3(crow)
        in_head_q = (lane_q >= HEAD_DIM * hh) & (lane_q < HEAD_DIM * (hh + 1))
        qa = jnp.where(in_head_q, qpair,
             jnp.where(lane_q == f0, q1, jnp.where(lane_q == f0 + 1, q2, jnp.where(lane_q == f0 + 2, q3,
             jnp.where((lane_q >= f0 + 3) & (lane_q < f0 + 6), 1.0, 0.0)))))
        in_head_k = (row_k >= HEAD_DIM * hh) & (row_k < HEAD_DIM * (hh + 1))
        ka = jnp.where(in_head_k, kpair,
             jnp.where((row_k >= f0) & (row_k < f0 + 3), 1.0,
             jnp.where(row_k == f0 + 3, -k1, jnp.where(row_k == f0 + 4, -k2, jnp.where(row_k == f0 + 5, -k3, 0.0)))))
        qa_ref[hh] = qa.astype(BF16)
        fill_cols(ka_ref.at[hh], ka.astype(BF16))

    def update(state, s, j):
        m, l, acc = state
        m_new = jnp.maximum(m, jnp.max(s, axis=1, keepdims=True))
        alpha = jnp.exp(m - m_new)
        p = jnp.exp(s - m_new)
        l = alpha * l + jnp.sum(p, axis=1, keepdims=True)
        acc = alpha * acc + _dot_nt(p.astype(BF16), va_ref[j])
        return m_new, l, acc

    n_q = max(seq_len // q_rows, 1)
    for i in range(n_q):
        r0 = i * q_rows
        r1 = seq_len if i == n_q - 1 else r0 + q_rows
        n = r1 - r0
        qb = [qa_ref[hh, r0:r1, :] for hh in range(2)]
        n_open = r0 // blk
        n_need = pl.cdiv(r1, blk)

        def open_block(j, states, qb=qb):
            return tuple(update(states[hh], _dot(qb[hh], ka_ref[hh, j]), j) for hh in range(2))

        init = (jnp.full((n, 1), NEG, F32), jnp.zeros((n, 1), F32), jnp.zeros((n, LANES), F32))
        states = lax.fori_loop(0, n_open, open_block, (init, init))
        row = r0 + lax.broadcasted_iota(jnp.int32, (n, blk), 0)
        col = lax.broadcasted_iota(jnp.int32, (n, blk), 1)
        for j in range(n_open, n_need):
            visible = (j * blk + col) <= row
            states = tuple(update(states[hh], jnp.where(visible, _dot(qb[hh], ka_ref[hh, j]), NEG), j)
                           for hh in range(2))
        (_, l0, acc0), (_, l1, acc1) = states
        lane_o = lax.broadcasted_iota(jnp.int32, (n, LANES), 1)
        o_ref[0, r0:r1, :] = jnp.where(lane_o < HEAD_DIM, acc0 / l0, acc1 / l1).astype(o_ref.dtype)


def _attn_prompt(q, kt, vt, ct, cc):
    b, seq_len, _ = q.shape
    blk = ATT_BLOCK
    nblk = pl.cdiv(seq_len, blk)
    return pl.pallas_call(
        functools.partial(_attn_body, blk=blk, q_rows=ATT_Q_ROWS),
        grid=(b, N_HEADS // 2),
        in_specs=[
            pl.BlockSpec((1, seq_len, LANES), lambda i, p: (i, 0, p)),
            pl.BlockSpec((1, LANES, seq_len), lambda i, p: (i, p, 0)),
            pl.BlockSpec((1, LANES, seq_len), lambda i, p: (i, p, 0)),
            pl.BlockSpec((1, N_HEADS, seq_len), lambda i, p: (i, 0, 0)),
            pl.BlockSpec((1, seq_len, N_HEADS), lambda i, p: (i, 0, 0)),
        ],
        out_specs=pl.BlockSpec((1, seq_len, LANES), lambda i, p: (i, 0, p)),
        out_shape=jax.ShapeDtypeStruct((b, seq_len, ATT_W), BF16),
        scratch_shapes=[
            pltpu.VMEM((2, seq_len, LANES), BF16),
            pltpu.VMEM((2, nblk, LANES, blk), BF16),
            pltpu.VMEM((nblk, LANES, blk), BF16),
        ],
        compiler_params=_cparams("arbitrary", "arbitrary"),
        name="attn_prompt",
    )(q, kt, vt, ct, cc)


def _decode_body(pt_ref, q_ref, kn_ref, vn_ref, lfn_ref, *rest, n_pg, t_new):
    k_refs = rest[0:n_pg]
    v_refs = rest[n_pg:2 * n_pg]
    f_refs = rest[2 * n_pg:3 * n_pg]
    o_ref = rest[3 * n_pg]
    qbd_ref, m_ref, l_ref, acc_ref, carry_ref, cnew_ref = rest[3 * n_pg + 1:]
    j = pl.program_id(1)
    n_rows = t_new * N_HEADS

    row_h = lax.broadcasted_iota(jnp.int32, (n_rows, ATT_W), 0) & (N_HEADS - 1)
    own = (lax.broadcasted_iota(jnp.int32, (n_rows, ATT_W), 1) >> HEAD_SHIFT) == row_h

    r_i = lax.broadcasted_iota(jnp.int32, (n_rows, n_rows), 0)
    c_i = lax.broadcasted_iota(jnp.int32, (n_rows, n_rows), 1)
    same_head = (r_i & (N_HEADS - 1)) == (c_i & (N_HEADS - 1))

    def new_token_sum(lo):
        sel = same_head & ((c_i >> TOK_SHIFT) > lo) & ((c_i >> TOK_SHIFT) <= (r_i >> TOK_SHIFT))
        return jnp.sum(jnp.where(sel, lfn_ref[0], 0.0), axis=1, keepdims=True)

    @pl.when(j == 0)
    def _():
        q4 = q_ref[0]
        rep = jnp.concatenate([jnp.broadcast_to(q4[t:t + 1], (N_HEADS, ATT_W)) for t in range(t_new)], axis=0)
        qbd_ref[...] = jnp.where(own, rep, 0.0).astype(BF16)
        m_ref[...] = jnp.full_like(m_ref, NEG)
        l_ref[...] = jnp.zeros_like(l_ref)
        acc_ref[...] = jnp.zeros_like(acc_ref)
        carry_ref[...] = jnp.zeros_like(carry_ref)
        cnew_ref[...] = new_token_sum(-1)

    def update(s, pv):
        m = m_ref[...]
        m_new = jnp.maximum(m, jnp.max(s, axis=1, keepdims=True))
        alpha = jnp.exp(m - m_new)
        p = jnp.exp(s - m_new)
        l_ref[...] = alpha * l_ref[...] + jnp.sum(p, axis=1, keepdims=True)
        p = p.astype(BF16)
        out = pv(p[:, 0:PAGE], 0)
        for k in range(1, s.shape[1] // PAGE):
            out = out + pv(p[:, k * PAGE:(k + 1) * PAGE], k)
        acc_ref[...] = alpha * acc_ref[...] + out
        m_ref[...] = m_new

    later = (lax.broadcasted_iota(jnp.int32, (PAGE, PAGE), 0) >
             lax.broadcasted_iota(jnp.int32, (PAGE, PAGE), 1)).astype(BF16)
    qbd = qbd_ref[...]
    f_all = jnp.concatenate([f_refs[p][...] for p in range(n_pg)], axis=0)
    a, b, c = _split3(f_all)
    cs = _dot(jnp.concatenate([a, b, c], axis=0).astype(BF16), later)
    nf = n_pg * N_HEADS
    within = cs[0:nf] + cs[nf:2 * nf] + cs[2 * nf:3 * nf]
    total = jnp.sum(f_all, axis=1, keepdims=True)
    carry = carry_ref[...]
    tiles = []
    for p in range(n_pg):
        rows = slice(p * N_HEADS, (p + 1) * N_HEADS)
        suffix = within[rows] + carry
        carry = carry + total[rows]
        kt = k_refs[p][...].reshape(ATT_W, PAGE).astype(BF16)
        tiles.append(_dot(qbd, kt) + jnp.concatenate([suffix] * t_new, axis=0))
    carry_ref[...] = carry
    s = jnp.concatenate(tiles, axis=1) + cnew_ref[...]
    update(s, lambda pb, p: _dot_nt(pb, v_refs[p][...].reshape(ATT_W, PAGE).astype(BF16)))

    @pl.when(j == pl.num_programs(1) - 1)
    def _():
        zpad = jnp.zeros((PAGE - t_new, ATT_W), F32)
        kpad = jnp.concatenate([kn_ref[0], zpad], axis=0).astype(BF16)
        vpad = jnp.concatenate([vn_ref[0], zpad], axis=0).astype(BF16)
        lane = lax.broadcasted_iota(jnp.int32, (n_rows, PAGE), 1)
        tok = lax.broadcasted_iota(jnp.int32, (n_rows, PAGE), 0) >> TOK_SHIFT
        bias = jnp.zeros((n_rows, PAGE), F32)
        for t in range(t_new):
            bias = jnp.where(lane == t, new_token_sum(t), bias)
        s = jnp.where(lane <= tok, _dot_nt(qbd, kpad) + bias, NEG)
        update(s, lambda pb, k: _dot(pb, vpad))
        out = jnp.where(own, acc_ref[...] / l_ref[...], 0.0)
        o_ref[0] = jnp.sum(out.reshape(t_new, N_HEADS, ATT_W), axis=1)


def _decode(page_table, q, k_new, v_new, lf_new, ckt, cvt, cft, layer):
    nb, t_new, _ = q.shape
    n_pages = page_table.shape[1]
    n_pg = DEC_PAGES_PER_STEP
    n_rows = t_new * N_HEADS

    def page_spec(p, shape):
        zeros = (0,) * len(shape)

        def index(b, j, pt):
            return (layer, pt[b, n_pages - 1 - (j * n_pg + p)]) + zeros

        return pl.BlockSpec((None, None) + shape, index)

    new = lambda b, j, pt: (b, 0, 0)
    in_specs = [pl.BlockSpec((1, t_new, ATT_W), new), pl.BlockSpec((1, t_new, ATT_W), new),
                pl.BlockSpec((1, t_new, ATT_W), new), pl.BlockSpec((1, 1, n_rows), new)]
    in_specs += [page_spec(p, (N_HEADS, HEAD_DIM, PAGE)) for p in range(n_pg)]
    in_specs += [page_spec(p, (N_HEADS, HEAD_DIM, PAGE)) for p in range(n_pg)]
    in_specs += [page_spec(p, (N_HEADS, PAGE)) for p in range(n_pg)]
    grid_spec = pltpu.PrefetchScalarGridSpec(
        num_scalar_prefetch=1,
        grid=(nb, n_pages // n_pg),
        in_specs=in_specs,
        out_specs=pl.BlockSpec((1, t_new, ATT_W), new),
        scratch_shapes=[
            pltpu.VMEM((n_rows, ATT_W), BF16),
            pltpu.VMEM((n_rows, 1), F32),
            pltpu.VMEM((n_rows, 1), F32),
            pltpu.VMEM((n_rows, ATT_W), F32),
            pltpu.VMEM((N_HEADS, 1), F32),
            pltpu.VMEM((n_rows, 1), F32),
        ],
    )
    return pl.pallas_call(
        functools.partial(_decode_body, n_pg=n_pg, t_new=t_new),
        grid_spec=grid_spec,
        out_shape=jax.ShapeDtypeStruct((nb, t_new, ATT_W), F32),
        compiler_params=_cparams("arbitrary", "arbitrary"),
        name="attn_decode",
    )(page_table, q, k_new, v_new, lf_new, *([ckt] * n_pg), *([cvt] * n_pg), *([cft] * n_pg))


def _merge_body(x_ref, pu_ref, ao_ref, gpre_ref, wg_ref, wpu_ref, wau_ref, wout_ref, gpost_ref, o_ref, m_ref,
                *, chunk):
    x = x_ref[...]
    d = x.shape[1]
    h = _rms(x, gpre_ref[...]).astype(BF16)
    pu = pu_ref[...].astype(BF16)
    ao = ao_ref[...].astype(BF16)
    for c in range(d // chunk):
        sl = slice(c * chunk, (c + 1) * chunk)
        sl_att = slice(d + c * chunk, d + (c + 1) * chunk)
        m = (jax.nn.sigmoid(_dot(h, wg_ref[:, sl])) * _dot(pu, wpu_ref[:, sl])
             + jax.nn.sigmoid(_dot(h, wg_ref[:, sl_att])) * _dot(ao, wau_ref[:, sl]))
        m_ref[:, sl] = m.astype(BF16)
    y = _dot(m_ref[...], wout_ref[...])
    o_ref[...] = x + _rms(y, gpost_ref[...])


def _merge(x, pu, ao, g_pre, wg, wpu, wau, wout, g_post, layer, tm):
    n, d = x.shape
    row = lambda i: (i, 0)
    const = lambda i: (layer, 0, 0)
    return pl.pallas_call(
        functools.partial(_merge_body, chunk=256),
        grid=(n // tm,),
        in_specs=[
            pl.BlockSpec((tm, d), row),
            pl.BlockSpec((tm, POOL_W), row),
            pl.BlockSpec((tm, ATT_W), row),
            pl.BlockSpec((None, 1, d), const),
            pl.BlockSpec((None, d, 2 * d), const),
            pl.BlockSpec((None, POOL_W, d), const),
            pl.BlockSpec((None, ATT_W, d), const),
            pl.BlockSpec((None, d, d), const),
            pl.BlockSpec((None, 1, d), const),
        ],
        out_specs=pl.BlockSpec((tm, d), row),
        out_shape=jax.ShapeDtypeStruct((n, d), F32),
        scratch_shapes=[pltpu.VMEM((tm, d), BF16)],
        compiler_params=_cparams("arbitrary"),
        name="merge",
    )(x, pu, ao, g_pre, wg, wpu, wau, wout, g_post)


def _row_tile(n, cap):
    best = None
    for t in range(16, cap + 1, 16):
        if n % t == 0:
            best = t
    assert best is not None, n
    return best


def kernel(x_prompt, x_sample, cache_k, cache_v, cache_logf, state_pool, page_table, meta_tokens,
           ln_ffn1_pre, ln_ffn1_post, ffn1_wi, ffn1_wo, ln_mix_pre, ln_mix_post, w_in, b_forget,
           pool_w_grp, pool_scale, w_pool_up, w_att_up, w_out, ln_ffn2_pre, ln_ffn2_post,
           ffn2_wi, ffn2_wo):
    nbp, seq, d = x_prompt.shape
    nbs, t_new, _ = x_sample.shape
    depth = w_in.shape[0]
    n_pages = page_table.shape[1]
    past_len = n_pages * PAGE
    seq_len = seq + N_META

    meta = jnp.broadcast_to(meta_tokens.astype(x_prompt.dtype)[None], (nbp, N_META, d))
    xp = jnp.concatenate([meta, x_prompt], axis=1).reshape(nbp * seq_len, d)
    xs = x_sample.reshape(nbs * t_new, d)
    tm_p = _row_tile(nbp * seq_len, 768)
    tm_s = nbs * t_new

    bf = lambda w: w.astype(BF16)
    vec = lambda g: g[:, None, :]
    o1, o2, o3, o4, o5 = POOL_W, POOL_W + ATT_W, POOL_W + 2 * ATT_W, POOL_W + 3 * ATT_W, POOL_W + 3 * ATT_W + N_HEADS
    w_in_b = bf(w_in)
    w_f_pad = jnp.pad(w_in_b[:, :, o4:o5], ((0, 0), (0, 0), (0, LANES - N_HEADS)))
    w_rows_p = w_in_b[:, :, 0:o2]
    w_rows_s = jnp.concatenate([w_in_b[:, :, 0:o4], w_f_pad], axis=2)
    w_gate_b = w_in_b[:, :, o5:]
    w_kv_t = jnp.transpose(w_in_b[:, :, o2:o4], (0, 2, 1))
    w_f_t = jnp.pad(jnp.transpose(w_in_b[:, :, o4:o5], (0, 2, 1)), ((0, 0), (0, 16 - N_HEADS), (0, 0)))
    bf_row = jnp.pad(b_forget, ((0, 0), (0, LANES - N_HEADS)))[:, None, :]
    bf_col = b_forget[:, :, None]
    ffn1_wi_b, ffn1_wo_b, ffn2_wi_b, ffn2_wo_b = bf(ffn1_wi), bf(ffn1_wo), bf(ffn2_wi), bf(ffn2_wo)
    w_grp_b, wpu_b, wau_b, wout_b = bf(pool_w_grp), bf(w_pool_up), bf(w_att_up), bf(w_out)
    scale3 = vec(pool_scale)
    g1pre, g1post, gmpre, gmpost, g2pre, g2post = (vec(g) for g in (
        ln_ffn1_pre, ln_ffn1_post, ln_mix_pre, ln_mix_post, ln_ffn2_pre, ln_ffn2_post))

    ckt = jnp.transpose(cache_k, (0, 1, 3, 4, 2))
    cvt = jnp.transpose(cache_v, (0, 1, 3, 4, 2))
    cft = jnp.transpose(cache_logf, (0, 1, 3, 2))
    hist = jnp.transpose(state_pool, (0, 2, 1, 3))

    kp_l, vp_l, fp_l, pp_l, ks_l, vs_l, fs_l, ps_l = [], [], [], [], [], [], [], []
    for l in range(depth):
        xp = _ffn(xp, g1pre, g1post, ffn1_wi_b, ffn1_wo_b, l, tm_p)
        xs = _ffn(xs, g1pre, g1post, ffn1_wi_b, ffn1_wo_b, l, tm_s)

        u, q = _inproj_rows(xp, gmpre, w_rows_p, bf_row, l, tm_p, with_kv=False)
        kt, vt, lft, ct = _inproj_t(xp.reshape(nbp, seq_len, d), gmpre, w_kv_t, w_f_t, bf_col, l, KV_TILE)
        u3 = u.reshape(nbp, seq_len, POOL_W)
        pu = _pool_seq(u3, w_grp_b, scale3, l)
        ao = _attn_prompt(q.reshape(nbp, seq_len, ATT_W), kt, vt, ct, jnp.transpose(ct, (0, 2, 1)))
        xp = _merge(xp, pu.reshape(-1, POOL_W), ao.reshape(-1, ATT_W), gmpre, w_gate_b, wpu_b, wau_b, wout_b,
                    gmpost, l, tm_p)
        kp_l.append(kt); vp_l.append(vt); fp_l.append(lft); pp_l.append(u3[:, seq_len - POOL_HIST:])

        u, q, k, v, lf = _inproj_rows(xs, gmpre, w_rows_s, bf_row, l, tm_s, with_kv=True)
        u_tb = jnp.transpose(u.reshape(nbs, t_new, POOL_W), (1, 0, 2))
        pu = jnp.transpose(_pool_step(hist, u_tb, w_grp_b, scale3, l, past_len), (1, 0, 2))
        lf8 = lf[:, 0:N_HEADS].reshape(nbs, t_new, N_HEADS)
        ao = _decode(page_table, q.reshape(nbs, t_new, ATT_W), k.reshape(nbs, t_new, ATT_W),
                     v.reshape(nbs, t_new, ATT_W), lf8.reshape(nbs, 1, t_new * N_HEADS), ckt, cvt, cft, l)
        xs = _merge(xs, pu.reshape(-1, POOL_W), ao.reshape(-1, ATT_W), gmpre, w_gate_b, wpu_b, wau_b, wout_b,
                    gmpost, l, tm_s)
        ks_l.append(k.reshape(nbs, t_new, N_HEADS, HEAD_DIM)); vs_l.append(v.reshape(nbs, t_new, N_HEADS, HEAD_DIM))
        fs_l.append(lf8)
        ps_l.append(jnp.concatenate([state_pool[l], u.reshape(nbs, t_new, POOL_W)], axis=1)[:, t_new:])

        xp = _ffn(xp, g2pre, g2post, ffn2_wi_b, ffn2_wo_b, l, tm_p)
        xs = _ffn(xs, g2pre, g2post, ffn2_wi_b, ffn2_wo_b, l, tm_s)

    y_prompt = xp.reshape(nbp, seq_len, d)[:, N_META:]
    y_sample = xs.reshape(nbs, t_new, d)
    heads_t = lambda a: jnp.transpose(jnp.stack(a).reshape(depth, nbp, N_HEADS, HEAD_DIM, seq_len), (0, 1, 4, 2, 3))
    return (y_prompt, y_sample,
            heads_t(kp_l), heads_t(vp_l), jnp.transpose(jnp.stack(fp_l), (0, 1, 3, 2)), jnp.stack(pp_l),
            jnp.stack(ks_l), jnp.stack(vs_l), jnp.stack(fs_l), jnp.stack(ps_l))
```

```python
import functools

import jax
import jax.numpy as jnp
from jax import lax
from jax.experimental import pallas as pl
from jax.experimental.pallas import tpu as pltpu

F32 = jnp.float32
BF16 = jnp.bfloat16

RMS_EPS = 1e-6
N_META = 16
N_HEADS = 8
HEAD_DIM = 64
HEAD_SHIFT = 6
TOK_SHIFT = 3
ATT_W = N_HEADS * HEAD_DIM
POOL_WINDOWS = (2, 4, 8, 16)
POOL_GROUP_W = 128
POOL_W = len(POOL_WINDOWS) * POOL_GROUP_W
POOL_HIST = max(POOL_WINDOWS) - 1
HIST_ROWS = 16
PAGE = 128
LANES = 128
NEG = float(jnp.finfo(jnp.float32).min)

ATT_BLOCK = 512
ATT_Q_ROWS = 512
KV_TILE = 512
DEC_PAGES_PER_STEP = 16
VMEM_LIMIT = 56 * 1024 * 1024


def _cparams(*sem):
    return pltpu.CompilerParams(dimension_semantics=sem, vmem_limit_bytes=VMEM_LIMIT)


def _rms(x, g):
    ms = jnp.mean(x * x, axis=-1, keepdims=True)
    return x * lax.rsqrt(ms + RMS_EPS) * g


def _log_sigmoid(x):
    return -(jnp.maximum(-x, 0.0) + jnp.log1p(jnp.exp(-jnp.abs(x))))


def _split3(x):
    a = x.astype(BF16).astype(F32)
    r = x - a
    b = r.astype(BF16).astype(F32)
    c = (r - b).astype(BF16).astype(F32)
    return a, b, c


def _dot(a, b):
    return jnp.dot(a, b, preferred_element_type=F32)


def _dot_nt(a, b):
    return lax.dot_general(a, b, (((1,), (1,)), ((), ())), preferred_element_type=F32)


def _ffn_body(x_ref, gpre_ref, gpost_ref, wig_ref, wiu_ref, wo_ref, o_ref, a_ref, *, chunk):
    x = x_ref[...]
    h = _rms(x, gpre_ref[...]).astype(BF16)
    d_ff = a_ref.shape[1]
    for c in range(d_ff // chunk):
        sl = slice(c * chunk, (c + 1) * chunk)
        g = _dot(h, wig_ref[:, sl])
        u = _dot(h, wiu_ref[:, sl])
        a_ref[:, sl] = (g * jax.nn.sigmoid(g) * u).astype(BF16)
    y = _dot(a_ref[...], wo_ref[...])
    o_ref[...] = x + 0.5 * _rms(y, gpost_ref[...])


def _ffn(x, g_pre, g_post, wi, wo, layer, tm):
    n, d = x.shape
    d_ff = wo.shape[1]
    const = lambda i: (layer, 0, 0)
    return pl.pallas_call(
        functools.partial(_ffn_body, chunk=256),
        grid=(n // tm,),
        in_specs=[
            pl.BlockSpec((tm, d), lambda i: (i, 0)),
            pl.BlockSpec((None, 1, d), const),
            pl.BlockSpec((None, 1, d), const),
            pl.BlockSpec((None, d, d_ff), const),
            pl.BlockSpec((None, d, d_ff), lambda i: (layer, 0, 1)),
            pl.BlockSpec((None, d_ff, d), const),
        ],
        out_specs=pl.BlockSpec((tm, d), lambda i: (i, 0)),
        out_shape=jax.ShapeDtypeStruct((n, d), F32),
        scratch_shapes=[pltpu.VMEM((tm, d_ff), BF16)],
        compiler_params=_cparams("arbitrary"),
        name="ffn_half_step",
    )(x, g_pre, g_post, wi, wi, wo)


def _inproj_rows_body(x_ref, g_ref, w_ref, bf_ref, u_ref, q_ref, k_ref, v_ref, lf_ref):
    h = _rms(x_ref[...], g_ref[...]).astype(BF16)
    u_ref[...] = _dot(h, w_ref[:, 0:POOL_W])
    q_ref[...] = _dot(h, w_ref[:, POOL_W:POOL_W + ATT_W]) * (HEAD_DIM ** -0.5)
    k0 = POOL_W + ATT_W
    k_ref[...] = _dot(h, w_ref[:, k0:k0 + ATT_W])
    v_ref[...] = _dot(h, w_ref[:, k0 + ATT_W:k0 + 2 * ATT_W])
    zf = _dot(h, w_ref[:, k0 + 2 * ATT_W:k0 + 2 * ATT_W + LANES])
    lf_ref[...] = _log_sigmoid(zf + bf_ref[...])


def _inproj_rows(x, g, w, bf, layer, tm):
    n, d = x.shape
    wcols = w.shape[2]
    const = lambda i: (layer, 0, 0)
    row = lambda i: (i, 0)
    widths = (POOL_W, ATT_W, ATT_W, ATT_W, LANES)
    return pl.pallas_call(
        _inproj_rows_body,
        grid=(n // tm,),
        in_specs=[
            pl.BlockSpec((tm, d), row),
            pl.BlockSpec((None, 1, d), const),
            pl.BlockSpec((None, d, wcols), const),
            pl.BlockSpec((None, 1, LANES), const),
        ],
        out_specs=[pl.BlockSpec((tm, w_), row) for w_ in widths],
        out_shape=[jax.ShapeDtypeStruct((n, w_), F32) for w_ in widths],
        compiler_params=_cparams("arbitrary"),
        name="inproj_rows",
    )(x, g, w, bf)


def _inproj_t_body(x_ref, g_ref, wt_ref, wft_ref, bft_ref, tri_ref, qt_ref, kt_ref, vt_ref, lft_ref, ct_ref,
                   carry_ref, *, seq_len):
    j = pl.program_id(1)
    tl = x_ref.shape[1]
    h = _rms(x_ref[0], g_ref[...]).astype(BF16)
    zt = _dot_nt(wt_ref[...], h)
    qt_ref[0] = (zt[0:ATT_W] * (HEAD_DIM ** -0.5)).astype(qt_ref.dtype)
    kt_ref[0] = zt[ATT_W:2 * ATT_W]
    vt_ref[0] = zt[2 * ATT_W:3 * ATT_W]
    zf = _dot_nt(wft_ref[...], h)[0:N_HEADS]
    lf = _log_sigmoid(zf + bft_ref[...])
    lft_ref[0] = lf

    @pl.when(j == 0)
    def _():
        carry_ref[...] = jnp.zeros_like(carry_ref)

    pos = j * tl + lax.broadcasted_iota(jnp.int32, lf.shape, 1)
    lf = jnp.where(pos < seq_len, lf, 0.0)
    a, b, c = _split3(lf)
    stack = jnp.concatenate([a, b, c, jnp.zeros_like(a)], axis=0).astype(BF16)
    cs = _dot(stack, tri_ref[...])
    run = cs[0:8] + cs[8:16] + cs[16:24] + carry_ref[...]
    ct_ref[0] = run
    carry_ref[...] = run[:, tl - 1:tl]


def _inproj_t(xp, g, wt, wft, bft, layer, tl):
    b, seq_len, d = xp.shape
    nt = pl.cdiv(seq_len, tl)
    tri = (lax.broadcasted_iota(jnp.int32, (tl, tl), 0) <= lax.broadcasted_iota(jnp.int32, (tl, tl), 1)).astype(BF16)
    const = lambda i, j: (layer, 0, 0)
    tile = lambda i, j: (i, 0, j)
    return pl.pallas_call(
        functools.partial(_inproj_t_body, seq_len=seq_len),
        grid=(b, nt),
        in_specs=[
            pl.BlockSpec((1, tl, d), lambda i, j: (i, j, 0)),
            pl.BlockSpec((None, 1, d), const),
            pl.BlockSpec((None, 3 * ATT_W, d), const),
            pl.BlockSpec((None, 16, d), const),
            pl.BlockSpec((None, N_HEADS, 1), const),
            pl.BlockSpec((tl, tl), lambda i, j: (0, 0)),
        ],
        out_specs=[
            pl.BlockSpec((1, ATT_W, tl), tile),
            pl.BlockSpec((1, ATT_W, tl), tile),
            pl.BlockSpec((1, ATT_W, tl), tile),
            pl.BlockSpec((1, N_HEADS, tl), tile),
            pl.BlockSpec((1, N_HEADS, tl), tile),
        ],
        out_shape=[
            jax.ShapeDtypeStruct((b, ATT_W, seq_len), BF16),
            jax.ShapeDtypeStruct((b, ATT_W, seq_len), F32),
            jax.ShapeDtypeStruct((b, ATT_W, seq_len), F32),
            jax.ShapeDtypeStruct((b, N_HEADS, seq_len), F32),
            jax.ShapeDtypeStruct((b, N_HEADS, seq_len), F32),
        ],
        scratch_shapes=[pltpu.VMEM((N_HEADS, 1), F32)],
        compiler_params=_cparams("arbitrary", "arbitrary"),
        name="inproj_feature_major",
    )(xp, g, wt, wft, bft, tri)


def _pool_mix_group(tot, cur, cnt, w_ref, scale_ref, g):
    cs = slice(g * POOL_GROUP_W, (g + 1) * POOL_GROUP_W)
    d = tot / cnt - cur
    return _dot(d.astype(BF16), w_ref[g]) * scale_ref[:, cs]


def _pool_seq_body(x_ref, g_ref, wu_ref, w_ref, scale_ref, o_ref, tail_ref, pad_ref, *, chunk):
    seq_len = x_ref.shape[1]
    pad_ref[0:HIST_ROWS, :] = jnp.zeros((HIST_ROWS, POOL_W), F32)
    for r0 in range(0, seq_len, chunk):
        h = _rms(x_ref[0, r0:r0 + chunk, :], g_ref[...]).astype(BF16)
        pad_ref[HIST_ROWS + r0:HIST_ROWS + r0 + chunk, :] = _dot(h, wu_ref[...])
    tail_ref[0] = pad_ref[HIST_ROWS + seq_len - POOL_HIST:HIST_ROWS + seq_len, :]
    for r0 in range(0, seq_len, chunk):
        pos = r0 + lax.broadcasted_iota(jnp.int32, (chunk, 1), 0)
        for g, w in enumerate(POOL_WINDOWS):
            cs = slice(g * POOL_GROUP_W, (g + 1) * POOL_GROUP_W)
            base = HIST_ROWS + r0
            cur = pad_ref[base:base + chunk, cs]
            tot = cur
            for i in range(1, w):
                tot = tot + pad_ref[base - i:base - i + chunk, cs]
            cnt = jnp.minimum(w, pos + 1).astype(F32)
            o_ref[0, r0:r0 + chunk, cs] = _pool_mix_group(tot, cur, cnt, w_ref, scale_ref, g).astype(o_ref.dtype)


def _pool_seq(x, g, wu, w_grp, scale, layer):
    b, seq_len, d = x.shape
    chunk = seq_len // 3
    const3 = lambda i: (layer, 0, 0)
    return pl.pallas_call(
        functools.partial(_pool_seq_body, chunk=chunk),
        grid=(b,),
        in_specs=[
            pl.BlockSpec((1, seq_len, d), lambda i: (i, 0, 0)),
            pl.BlockSpec((None, 1, d), const3),
            pl.BlockSpec((None, d, POOL_W), const3),
            pl.BlockSpec((None, len(POOL_WINDOWS), POOL_GROUP_W, POOL_GROUP_W), lambda i: (layer, 0, 0, 0)),
            pl.BlockSpec((None, 1, POOL_W), const3),
        ],
        out_specs=[pl.BlockSpec((1, seq_len, POOL_W), lambda i: (i, 0, 0)),
                   pl.BlockSpec((1, POOL_HIST, POOL_W), lambda i: (i, 0, 0))],
        out_shape=[jax.ShapeDtypeStruct((b, seq_len, POOL_W), BF16),
                   jax.ShapeDtypeStruct((b, POOL_HIST, POOL_W), F32)],
        scratch_shapes=[pltpu.VMEM((HIST_ROWS + seq_len, POOL_W), F32)],
        compiler_params=_cparams("arbitrary"),
        name="pool_prompt",
    )(x, g, wu, w_grp, scale)


def _pool_step_body(hist_ref, u_ref, w_ref, scale_ref, o_ref, *, pos0):
    n_hist = hist_ref.shape[1]
    t_new = u_ref.shape[0]

    def slab(i):
        return hist_ref[0, i] if i < n_hist else u_ref[i - n_hist]

    for t in range(t_new):
        end = n_hist + t
        for g, w in enumerate(POOL_WINDOWS):
            cs = slice(g * POOL_GROUP_W, (g + 1) * POOL_GROUP_W)
            cur = slab(end)[:, cs]
            tot = cur
            for i in range(1, w):
                if end - i >= 0:
                    tot = tot + slab(end - i)[:, cs]
            cnt = float(min(w, pos0 + t + 1))
            o_ref[t, :, cs] = _pool_mix_group(tot, cur, cnt, w_ref, scale_ref, g).astype(o_ref.dtype)


def _pool_step(hist, u_new, w_grp, scale, layer, pos0):
    _, n_hist, nb, _ = hist.shape
    t_new = u_new.shape[0]
    return pl.pallas_call(
        functools.partial(_pool_step_body, pos0=pos0),
        grid=(1,),
        in_specs=[
            pl.BlockSpec((1, n_hist, nb, POOL_W), lambda i: (layer, 0, 0, 0)),
            pl.BlockSpec((t_new, nb, POOL_W), lambda i: (0, 0, 0)),
            pl.BlockSpec((None, len(POOL_WINDOWS), POOL_GROUP_W, POOL_GROUP_W), lambda i: (layer, 0, 0, 0)),
            pl.BlockSpec((None, 1, POOL_W), lambda i: (layer, 0, 0)),
        ],
        out_specs=pl.BlockSpec((t_new, nb, POOL_W), lambda i: (0, 0, 0)),
        out_shape=jax.ShapeDtypeStruct((t_new, nb, POOL_W), F32),
        compiler_params=_cparams("arbitrary"),
        name="pool_sample",
    )(hist, u_new, w_grp, scale)


def _attn_body(qt_ref, kt_ref, vt_ref, ct_ref, o_ref, qa_ref, ka_ref, va_ref, kpad_ref, *, blk, q_lanes):
    hp = pl.program_id(1)
    seq_len = qt_ref.shape[2]
    nblk = ka_ref.shape[1]
    q_pad = qa_ref.shape[2]

    row = lax.broadcasted_iota(jnp.int32, (LANES, seq_len), 0)
    qpair = qt_ref[0].astype(F32)
    kpair = kt_ref[0]
    vpair = vt_ref[0].astype(BF16)
    n_full = seq_len // blk
    tail = seq_len - n_full * blk
    assert nblk == n_full + 1 and 0 < tail

    for j in range(n_full):
        va_ref[j] = vpair[:, j * blk:(j + 1) * blk]
    va_ref[n_full] = jnp.zeros((LANES, blk), BF16)
    va_ref[n_full, :, 0:tail] = vpair[:, n_full * blk:seq_len]

    kpad_ref[:, n_full * blk:nblk * blk] = jnp.zeros((LANES, blk), F32)
    qa_ref[...] = jnp.zeros(qa_ref.shape, BF16)
    for hh in range(2):
        f0 = HEAD_DIM * (1 - hh)
        c1, c2, c3 = _split3(ct_ref[0, pl.ds(2 * hp + hh, 1), :])
        in_head = (row >= HEAD_DIM * hh) & (row < HEAD_DIM * (hh + 1))
        qa = jnp.where(in_head, qpair,
             jnp.where(row == f0, c1, jnp.where(row == f0 + 1, c2, jnp.where(row == f0 + 2, c3,
             jnp.where((row >= f0 + 3) & (row < f0 + 6), 1.0, 0.0)))))
        ka = jnp.where(in_head, kpair,
             jnp.where((row >= f0) & (row < f0 + 3), 1.0,
             jnp.where(row == f0 + 3, -c1, jnp.where(row == f0 + 4, -c2, jnp.where(row == f0 + 5, -c3, 0.0)))))
        qa_ref[hh, :, 0:seq_len] = qa.astype(BF16)
        kpad_ref[:, 0:seq_len] = ka
        for j in range(nblk):
            ka_ref[hh, j] = jnp.transpose(kpad_ref[:, j * blk:(j + 1) * blk]).astype(BF16)

    def update(state, s, j, hh):
        m, l, acc = state
        m_new = jnp.maximum(m, jnp.max(s, axis=0, keepdims=True))
        alpha = jnp.exp(m - m_new)
        p = jnp.exp(s - m_new)
        l = alpha * l + jnp.sum(p, axis=0, keepdims=True)
        acc = alpha * acc + _dot(va_ref[j, HEAD_DIM * hh:HEAD_DIM * (hh + 1), :], p.astype(BF16))
        return m_new, l, acc

    n_q = max(q_pad // q_lanes, 1)
    for i in range(n_q):
        q0 = i * q_lanes
        q1 = q_pad if i == n_q - 1 else q0 + q_lanes
        r1 = min(q1, seq_len)
        n = q1 - q0
        qb = [qa_ref[hh, :, q0:q1] for hh in range(2)]
        n_open = q0 // blk
        n_need = pl.cdiv(r1, blk)

        def open_block(j, states, qb=qb):
            return tuple(update(states[hh], _dot(ka_ref[hh, j], qb[hh]), j, hh) for hh in range(2))

        init = (jnp.full((1, n), NEG, F32), jnp.zeros((1, n), F32), jnp.zeros((HEAD_DIM, n), F32))
        states = lax.fori_loop(0, n_open, open_block, (init, init), unroll=True)
        key = lax.broadcasted_iota(jnp.int32, (blk, n), 0)
        query = q0 + lax.broadcasted_iota(jnp.int32, (blk, n), 1)
        for j in range(n_open, n_need):
            visible = (j * blk + key) <= query
            states = tuple(update(states[hh], jnp.where(visible, _dot(ka_ref[hh, j], qb[hh]), NEG), j, hh)
                           for hh in range(2))
        (_, l0, acc0), (_, l1, acc1) = states
        out = jnp.transpose(jnp.concatenate([acc0 / l0, acc1 / l1], axis=0))
        o_ref[0, q0:r1, :] = out[0:r1 - q0, :].astype(o_ref.dtype)


def _attn_prompt(qt, kt, vt, ct):
    b, _, seq_len = qt.shape
    blk = ATT_BLOCK
    nblk = pl.cdiv(seq_len, blk)
    q_pad = pl.cdiv(seq_len, LANES) * LANES
    pair = lambda i, p: (i, p, 0)
    return pl.pallas_call(
        functools.partial(_attn_body, blk=blk, q_lanes=ATT_Q_ROWS),
        grid=(b, N_HEADS // 2),
        in_specs=[
            pl.BlockSpec((1, LANES, seq_len), pair),
            pl.BlockSpec((1, LANES, seq_len), pair),
            pl.BlockSpec((1, LANES, seq_len), pair),
            pl.BlockSpec((1, N_HEADS, seq_len), lambda i, p: (i, 0, 0)),
        ],
        out_specs=pl.BlockSpec((1, seq_len, LANES), lambda i, p: (i, 0, p)),
        out_shape=jax.ShapeDtypeStruct((b, seq_len, ATT_W), BF16),
        scratch_shapes=[
            pltpu.VMEM((2, LANES, q_pad), BF16),
            pltpu.VMEM((2, nblk, blk, LANES), BF16),
            pltpu.VMEM((nblk, LANES, blk), BF16),
            pltpu.VMEM((LANES, nblk * blk), F32),
        ],
        compiler_params=_cparams("arbitrary", "arbitrary"),
        name="attn_prompt",
    )(qt, kt, vt, ct)


def _decode_body(pt_ref, q_ref, kn_ref, vn_ref, lfn_ref, *rest, n_pg, t_new):
    k_refs = rest[0:n_pg]
    v_refs = rest[n_pg:2 * n_pg]
    f_refs = rest[2 * n_pg:3 * n_pg]
    o_ref = rest[3 * n_pg]
    qbd_ref, m_ref, l_ref, acc_ref, carry_ref, cnew_ref = rest[3 * n_pg + 1:]
    j = pl.program_id(1)
    n_rows = t_new * N_HEADS

    row_h = lax.broadcasted_iota(jnp.int32, (n_rows, ATT_W), 0) & (N_HEADS - 1)
    own = (lax.broadcasted_iota(jnp.int32, (n_rows, ATT_W), 1) >> HEAD_SHIFT) == row_h

    r_i = lax.broadcasted_iota(jnp.int32, (n_rows, n_rows), 0)
    c_i = lax.broadcasted_iota(jnp.int32, (n_rows, n_rows), 1)
    same_head = (r_i & (N_HEADS - 1)) == (c_i & (N_HEADS - 1))

    def new_token_sum(lo):
        sel = same_head & ((c_i >> TOK_SHIFT) > lo) & ((c_i >> TOK_SHIFT) <= (r_i >> TOK_SHIFT))
        return jnp.sum(jnp.where(sel, lfn_ref[0], 0.0), axis=1, keepdims=True)

    @pl.when(j == 0)
    def _():
        q4 = q_ref[0]
        rep = jnp.concatenate([jnp.broadcast_to(q4[t:t + 1], (N_HEADS, ATT_W)) for t in range(t_new)], axis=0)
        qbd_ref[...] = jnp.where(own, rep, 0.0).astype(BF16)
        m_ref[...] = jnp.full_like(m_ref, NEG)
        l_ref[...] = jnp.zeros_like(l_ref)
        acc_ref[...] = jnp.zeros_like(acc_ref)
        carry_ref[...] = jnp.zeros_like(carry_ref)
        cnew_ref[...] = new_token_sum(-1)

    def update(s, pv):
        m = m_ref[...]
        m_new = jnp.maximum(m, jnp.max(s, axis=1, keepdims=True))
        alpha = jnp.exp(m - m_new)
        p = jnp.exp(s - m_new)
        l_ref[...] = alpha * l_ref[...] + jnp.sum(p, axis=1, keepdims=True)
        p = p.astype(BF16)
        out = pv(p[:, 0:PAGE], 0)
        for k in range(1, s.shape[1] // PAGE):
            out = out + pv(p[:, k * PAGE:(k + 1) * PAGE], k)
        acc_ref[...] = alpha * acc_ref[...] + out
        m_ref[...] = m_new

    later = (lax.broadcasted_iota(jnp.int32, (PAGE, PAGE), 0) >
             lax.broadcasted_iota(jnp.int32, (PAGE, PAGE), 1)).astype(BF16)
    qbd = qbd_ref[...]
    f_all = jnp.concatenate([f_refs[p][...] for p in range(n_pg)], axis=0)
    a, b, c = _split3(f_all)
    cs = _dot(jnp.concatenate([a, b, c], axis=0).astype(BF16), later)
    nf = n_pg * N_HEADS
    within = cs[0:nf] + cs[nf:2 * nf] + cs[2 * nf:3 * nf]
    total = jnp.sum(f_all, axis=1, keepdims=True)
    carry = carry_ref[...]
    tiles = []
    for p in range(n_pg):
        rows = slice(p * N_HEADS, (p + 1) * N_HEADS)
        suffix = within[rows] + carry
        carry = carry + total[rows]
        kt = k_refs[p][...].reshape(ATT_W, PAGE).astype(BF16)
        tiles.append(_dot(qbd, kt) + jnp.concatenate([suffix] * t_new, axis=0))
    carry_ref[...] = carry
    s = jnp.concatenate(tiles, axis=1) + cnew_ref[...]
    update(s, lambda pb, p: _dot_nt(pb, v_refs[p][...].reshape(ATT_W, PAGE).astype(BF16)))

    @pl.when(j == pl.num_programs(1) - 1)
    def _():
        zpad = jnp.zeros((PAGE - t_new, ATT_W), F32)
        kpad = jnp.concatenate([kn_ref[0], zpad], axis=0).astype(BF16)
        vpad = jnp.concatenate([vn_ref[0], zpad], axis=0).astype(BF16)
        lane = lax.broadcasted_iota(jnp.int32, (n_rows, PAGE), 1)
        tok = lax.broadcasted_iota(jnp.int32, (n_rows, PAGE), 0) >> TOK_SHIFT
        bias = jnp.zeros((n_rows, PAGE), F32)
        for t in range(t_new):
            bias = jnp.where(lane == t, new_token_sum(t), bias)
        s = jnp.where(lane <= tok, _dot_nt(qbd, kpad) + bias, NEG)
        update(s, lambda pb, k: _dot(pb, vpad))
        out = jnp.where(own, acc_ref[...] / l_ref[...], 0.0)
        o_ref[0] = jnp.sum(out.reshape(t_new, N_HEADS, ATT_W), axis=1)


def _decode(page_table, q, k_new, v_new, lf_new, ckt, cvt, cft, layer):
    nb, t_new, _ = q.shape
    n_pages = page_table.shape[1]
    n_pg = DEC_PAGES_PER_STEP
    n_rows = t_new * N_HEADS

    def page_spec(p, shape):
        zeros = (0,) * len(shape)

        def index(b, j, pt):
            return (layer, pt[b, n_pages - 1 - (j * n_pg + p)]) + zeros

        return pl.BlockSpec((None, None) + shape, index)

    new = lambda b, j, pt: (b, 0, 0)
    in_specs = [pl.BlockSpec((1, t_new, ATT_W), new), pl.BlockSpec((1, t_new, ATT_W), new),
                pl.BlockSpec((1, t_new, ATT_W), new), pl.BlockSpec((1, 1, n_rows), new)]
    in_specs += [page_spec(p, (N_HEADS, HEAD_DIM, PAGE)) for p in range(n_pg)]
    in_specs += [page_spec(p, (N_HEADS, HEAD_DIM, PAGE)) for p in range(n_pg)]
    in_specs += [page_spec(p, (N_HEADS, PAGE)) for p in range(n_pg)]
    grid_spec = pltpu.PrefetchScalarGridSpec(
        num_scalar_prefetch=1,
        grid=(nb, n_pages // n_pg),
        in_specs=in_specs,
        out_specs=pl.BlockSpec((1, t_new, ATT_W), new),
        scratch_shapes=[
            pltpu.VMEM((n_rows, ATT_W), BF16),
            pltpu.VMEM((n_rows, 1), F32),
            pltpu.VMEM((n_rows, 1), F32),
            pltpu.VMEM((n_rows, ATT_W), F32),
            pltpu.VMEM((N_HEADS, 1), F32),
            pltpu.VMEM((n_rows, 1), F32),
        ],
    )
    return pl.pallas_call(
        functools.partial(_decode_body, n_pg=n_pg, t_new=t_new),
        grid_spec=grid_spec,
        out_shape=jax.ShapeDtypeStruct((nb, t_new, ATT_W), F32),
        compiler_params=_cparams("arbitrary", "arbitrary"),
        name="attn_decode",
    )(page_table, q, k_new, v_new, lf_new, *([ckt] * n_pg), *([cvt] * n_pg), *([cft] * n_pg))


def _merge_body(x_ref, pu_ref, ao_ref, gpre_ref, wg_ref, wpu_ref, wau_ref, wout_ref, gpost_ref, o_ref, m_ref,
                *, chunk):
    x = x_ref[...]
    d = x.shape[1]
    h = _rms(x, gpre_ref[...]).astype(BF16)
    pu = pu_ref[...].astype(BF16)
    ao = ao_ref[...].astype(BF16)
    for c in range(d // chunk):
        sl = slice(c * chunk, (c + 1) * chunk)
        sl_att = slice(d + c * chunk, d + (c + 1) * chunk)
        m = (jax.nn.sigmoid(_dot(h, wg_ref[:, sl])) * _dot(pu, wpu_ref[:, sl])
             + jax.nn.sigmoid(_dot(h, wg_ref[:, sl_att])) * _dot(ao, wau_ref[:, sl]))
        m_ref[:, sl] = m.astype(BF16)
    y = _dot(m_ref[...], wout_ref[...])
    o_ref[...] = x + _rms(y, gpost_ref[...])


def _merge(x, pu, ao, g_pre, wg, wpu, wau, wout, g_post, layer, tm):
    n, d = x.shape
    row = lambda i: (i, 0)
    const = lambda i: (layer, 0, 0)
    return pl.pallas_call(
        functools.partial(_merge_body, chunk=256),
        grid=(n // tm,),
        in_specs=[
            pl.BlockSpec((tm, d), row),
            pl.BlockSpec((tm, POOL_W), row),
            pl.BlockSpec((tm, ATT_W), row),
            pl.BlockSpec((None, 1, d), const),
            pl.BlockSpec((None, d, 2 * d), const),
            pl.BlockSpec((None, POOL_W, d), const),
            pl.BlockSpec((None, ATT_W, d), const),
            pl.BlockSpec((None, d, d), const),
            pl.BlockSpec((None, 1, d), const),
        ],
        out_specs=pl.BlockSpec((tm, d), row),
        out_shape=jax.ShapeDtypeStruct((n, d), F32),
        scratch_shapes=[pltpu.VMEM((tm, d), BF16)],
        compiler_params=_cparams("arbitrary"),
        name="merge",
    )(x, pu, ao, g_pre, wg, wpu, wau, wout, g_post)


def _row_tile(n, cap):
    best = None
    for t in range(16, cap + 1, 16):
        if n % t == 0:
            best = t
    assert best is not None, n
    return best


def kernel(x_prompt, x_sample, cache_k, cache_v, cache_logf, state_pool, page_table, meta_tokens,
           ln_ffn1_pre, ln_ffn1_post, ffn1_wi, ffn1_wo, ln_mix_pre, ln_mix_post, w_in, b_forget,
           pool_w_grp, pool_scale, w_pool_up, w_att_up, w_out, ln_ffn2_pre, ln_ffn2_post,
           ffn2_wi, ffn2_wo):
    nbp, seq, d = x_prompt.shape
    nbs, t_new, _ = x_sample.shape
    depth = w_in.shape[0]
    n_pages = page_table.shape[1]
    past_len = n_pages * PAGE
    seq_len = seq + N_META

    meta = jnp.broadcast_to(meta_tokens.astype(x_prompt.dtype)[None], (nbp, N_META, d))
    xp = jnp.concatenate([meta, x_prompt], axis=1).reshape(nbp * seq_len, d)
    xs = x_sample.reshape(nbs * t_new, d)
    tm_p = _row_tile(nbp * seq_len, 768)
    tm_s = nbs * t_new

    bf = lambda w: w.astype(BF16)
    vec = lambda g: g[:, None, :]
    o1, o2, o3, o4, o5 = POOL_W, POOL_W + ATT_W, POOL_W + 2 * ATT_W, POOL_W + 3 * ATT_W, POOL_W + 3 * ATT_W + N_HEADS
    w_in_b = bf(w_in)
    w_f_pad = jnp.pad(w_in_b[:, :, o4:o5], ((0, 0), (0, 0), (0, LANES - N_HEADS)))
    w_u_b = w_in_b[:, :, 0:o1]
    w_rows_s = jnp.concatenate([w_in_b[:, :, 0:o4], w_f_pad], axis=2)
    w_gate_b = w_in_b[:, :, o5:]
    w_qkv_t = jnp.transpose(w_in_b[:, :, o1:o4], (0, 2, 1))
    w_f_t = jnp.pad(jnp.transpose(w_in_b[:, :, o4:o5], (0, 2, 1)), ((0, 0), (0, 16 - N_HEADS), (0, 0)))
    bf_row = jnp.pad(b_forget, ((0, 0), (0, LANES - N_HEADS)))[:, None, :]
    bf_col = b_forget[:, :, None]
    ffn1_wi_b, ffn1_wo_b, ffn2_wi_b, ffn2_wo_b = bf(ffn1_wi), bf(ffn1_wo), bf(ffn2_wi), bf(ffn2_wo)
    w_grp_b, wpu_b, wau_b, wout_b = bf(pool_w_grp), bf(w_pool_up), bf(w_att_up), bf(w_out)
    scale3 = vec(pool_scale)
    g1pre, g1post, gmpre, gmpost, g2pre, g2post = (vec(g) for g in (
        ln_ffn1_pre, ln_ffn1_post, ln_mix_pre, ln_mix_post, ln_ffn2_pre, ln_ffn2_post))

    ckt = jnp.transpose(cache_k, (0, 1, 3, 4, 2))
    cvt = jnp.transpose(cache_v, (0, 1, 3, 4, 2))
    cft = jnp.transpose(cache_logf, (0, 1, 3, 2))
    hist = jnp.transpose(state_pool, (0, 2, 1, 3))

    kp_l, vp_l, fp_l, pp_l, ks_l, vs_l, fs_l, ps_l = [], [], [], [], [], [], [], []
    for l in range(depth):
        xp = _ffn(xp, g1pre, g1post, ffn1_wi_b, ffn1_wo_b, l, tm_p)
        xs = _ffn(xs, g1pre, g1post, ffn1_wi_b, ffn1_wo_b, l, tm_s)

        xp3 = xp.reshape(nbp, seq_len, d)
        qt, kt, vt, lft, ct = _inproj_t(xp3, gmpre, w_qkv_t, w_f_t, bf_col, l, KV_TILE)
        pu, u_tail = _pool_seq(xp3, gmpre, w_u_b, w_grp_b, scale3, l)
        ao = _attn_prompt(qt, kt, vt, ct)
        xp = _merge(xp, pu.reshape(-1, POOL_W), ao.reshape(-1, ATT_W), gmpre, w_gate_b, wpu_b, wau_b, wout_b,
                    gmpost, l, tm_p)
        kp_l.append(kt); vp_l.append(vt); fp_l.append(lft); pp_l.append(u_tail)

        u, q, k, v, lf = _inproj_rows(xs, gmpre, w_rows_s, bf_row, l, tm_s)
        u_tb = jnp.transpose(u.reshape(nbs, t_new, POOL_W), (1, 0, 2))
        pu = jnp.transpose(_pool_step(hist, u_tb, w_grp_b, scale3, l, past_len), (1, 0, 2))
        lf8 = lf[:, 0:N_HEADS].reshape(nbs, t_new, N_HEADS)
        ao = _decode(page_table, q.reshape(nbs, t_new, ATT_W), k.reshape(nbs, t_new, ATT_W),
                     v.reshape(nbs, t_new, ATT_W), lf8.reshape(nbs, 1, t_new * N_HEADS), ckt, cvt, cft, l)
        xs = _merge(xs, pu.reshape(-1, POOL_W), ao.reshape(-1, ATT_W), gmpre, w_gate_b, wpu_b, wau_b, wout_b,
                    gmpost, l, tm_s)
        ks_l.append(k.reshape(nbs, t_new, N_HEADS, HEAD_DIM)); vs_l.append(v.reshape(nbs, t_new, N_HEADS, HEAD_DIM))
        fs_l.append(lf8)
        ps_l.append(jnp.concatenate([state_pool[l], u.reshape(nbs, t_new, POOL_W)], axis=1)[:, t_new:])

        xp = _ffn(xp, g2pre, g2post, ffn2_wi_b, ffn2_wo_b, l, tm_p)
        xs = _ffn(xs, g2pre, g2post, ffn2_wi_b, ffn2_wo_b, l, tm_s)

    y_prompt = xp.reshape(nbp, seq_len, d)[:, N_META:]
    y_sample = xs.reshape(nbs, t_new, d)
    heads_t = lambda a: jnp.transpose(jnp.stack(a).reshape(depth, nbp, N_HEADS, HEAD_DIM, seq_len), (0, 1, 4, 2, 3))
    return (y_prompt, y_sample,
            heads_t(kp_l), heads_t(vp_l), jnp.transpose(jnp.stack(fp_l), (0, 1, 3, 2)), jnp.stack(pp_l),
            jnp.stack(ks_l), jnp.stack(vs_l), jnp.stack(fs_l), jnp.stack(ps_l))
```

```python
import functools

import jax
import jax.numpy as jnp
from jax import lax
from jax.experimental import pallas as pl
from jax.experimental.pallas import tpu as pltpu

F32 = jnp.float32
BF16 = jnp.bfloat16

RMS_EPS = 1e-6
N_META = 16
N_HEADS = 8
HEAD_DIM = 64
HEAD_SHIFT = 6
TOK_SHIFT = 3
ATT_W = N_HEADS * HEAD_DIM
POOL_WINDOWS = (2, 4, 8, 16)
POOL_GROUP_W = 128
POOL_W = len(POOL_WINDOWS) * POOL_GROUP_W
POOL_HIST = max(POOL_WINDOWS) - 1
HIST_ROWS = 16
PAGE = 128
LANES = 128
NEG = float(jnp.finfo(jnp.float32).min)

ATT_BLOCK = 512
ATT_Q_ROWS = 512
KV_TILE = 512
DEC_PAGES_PER_STEP = 32
VMEM_LIMIT = 56 * 1024 * 1024


def _cparams(*sem):
    return pltpu.CompilerParams(dimension_semantics=sem, vmem_limit_bytes=VMEM_LIMIT)


def _rms(x, g):
    ms = jnp.mean(x * x, axis=-1, keepdims=True)
    return x * lax.rsqrt(ms + RMS_EPS) * g


def _log_sigmoid(x):
    return -(jnp.maximum(-x, 0.0) + jnp.log1p(jnp.exp(-jnp.abs(x))))


def _split3(x):
    a = x.astype(BF16).astype(F32)
    r = x - a
    b = r.astype(BF16).astype(F32)
    c = (r - b).astype(BF16).astype(F32)
    return a, b, c


def _dot(a, b):
    return jnp.dot(a, b, preferred_element_type=F32)


def _dot_nt(a, b):
    return lax.dot_general(a, b, (((1,), (1,)), ((), ())), preferred_element_type=F32)


def _ffn_body(x_ref, gpre_ref, gpost_ref, wig_ref, wiu_ref, wo_ref, o_ref, a_ref, *, chunk):
    x = x_ref[...]
    h = _rms(x, gpre_ref[...]).astype(BF16)
    d_ff = a_ref.shape[1]
    for c in range(d_ff // chunk):
        sl = slice(c * chunk, (c + 1) * chunk)
        g = _dot(h, wig_ref[:, sl])
        u = _dot(h, wiu_ref[:, sl])
        a_ref[:, sl] = (g * jax.nn.sigmoid(g) * u).astype(BF16)
    y = _dot(a_ref[...], wo_ref[...])
    o_ref[...] = x + 0.5 * _rms(y, gpost_ref[...])


def _ffn(x, g_pre, g_post, wi, wo, layer, tm):
    n, d = x.shape
    d_ff = wo.shape[1]
    const = lambda i: (layer, 0, 0)
    return pl.pallas_call(
        functools.partial(_ffn_body, chunk=256),
        grid=(n // tm,),
        in_specs=[
            pl.BlockSpec((tm, d), lambda i: (i, 0)),
            pl.BlockSpec((None, 1, d), const),
            pl.BlockSpec((None, 1, d), const),
            pl.BlockSpec((None, d, d_ff), const),
            pl.BlockSpec((None, d, d_ff), lambda i: (layer, 0, 1)),
            pl.BlockSpec((None, d_ff, d), const),
        ],
        out_specs=pl.BlockSpec((tm, d), lambda i: (i, 0)),
        out_shape=jax.ShapeDtypeStruct((n, d), F32),
        scratch_shapes=[pltpu.VMEM((tm, d_ff), BF16)],
        compiler_params=_cparams("arbitrary"),
        name="ffn_half_step",
    )(x, g_pre, g_post, wi, wi, wo)


def _inproj_rows_body(x_ref, g_ref, w_ref, bf_ref, u_ref, q_ref, k_ref, v_ref, lf_ref):
    h = _rms(x_ref[...], g_ref[...]).astype(BF16)
    u_ref[...] = _dot(h, w_ref[:, 0:POOL_W])
    q_ref[...] = _dot(h, w_ref[:, POOL_W:POOL_W + ATT_W]) * (HEAD_DIM ** -0.5)
    k0 = POOL_W + ATT_W
    k_ref[...] = _dot(h, w_ref[:, k0:k0 + ATT_W])
    v_ref[...] = _dot(h, w_ref[:, k0 + ATT_W:k0 + 2 * ATT_W])
    zf = _dot(h, w_ref[:, k0 + 2 * ATT_W:k0 + 2 * ATT_W + LANES])
    lf_ref[...] = _log_sigmoid(zf + bf_ref[...])


def _inproj_rows(x, g, w, bf, layer, tm):
    n, d = x.shape
    wcols = w.shape[2]
    const = lambda i: (layer, 0, 0)
    row = lambda i: (i, 0)
    widths = (POOL_W, ATT_W, ATT_W, ATT_W, LANES)
    return pl.pallas_call(
        _inproj_rows_body,
        grid=(n // tm,),
        in_specs=[
            pl.BlockSpec((tm, d), row),
            pl.BlockSpec((None, 1, d), const),
            pl.BlockSpec((None, d, wcols), const),
            pl.BlockSpec((None, 1, LANES), const),
        ],
        out_specs=[pl.BlockSpec((tm, w_), row) for w_ in widths],
        out_shape=[jax.ShapeDtypeStruct((n, w_), F32) for w_ in widths],
        compiler_params=_cparams("arbitrary"),
        name="inproj_rows",
    )(x, g, w, bf)


def _inproj_t_body(x_ref, g_ref, wt_ref, wft_ref, bft_ref, tri_ref, kall_ref, vall_ref,
                   qt_ref, kt_ref, vt_ref, lft_ref, ct_ref, *, tl):
    del kall_ref, vall_ref
    seq_len = x_ref.shape[1]
    n_full = seq_len // tl
    rem = seq_len - n_full * tl
    win = pl.cdiv(rem, LANES) * LANES
    assert win <= tl
    carry = jnp.zeros((N_HEADS, 1), F32)
    chunks = [(c * tl, tl, 0) for c in range(n_full)] + ([(seq_len - win, win, win - rem)] if rem else [])
    for p0, width, skip in chunks:
        h = _rms(x_ref[0, p0:p0 + width, :], g_ref[...]).astype(BF16)
        zt = _dot_nt(wt_ref[...], h)
        lf = _log_sigmoid(_dot_nt(wft_ref[...], h)[0:N_HEADS] + bft_ref[...])
        fresh = lax.broadcasted_iota(jnp.int32, lf.shape, 1) >= skip
        a, b, c = _split3(jnp.where(fresh, lf, 0.0))
        stack = jnp.concatenate([a, b, c, jnp.zeros_like(a)], axis=0).astype(BF16)
        cs = _dot(stack, tri_ref[0:width, 0:width])
        run = cs[0:8] + cs[8:16] + cs[16:24] + carry
        carry = run[:, width - 1:width]
        dst = slice(p0 + skip, p0 + width)
        qt_ref[0, :, dst] = (zt[0:ATT_W, skip:] * (HEAD_DIM ** -0.5)).astype(qt_ref.dtype)
        kt_ref[0, :, dst] = zt[ATT_W:2 * ATT_W, skip:]
        vt_ref[0, :, dst] = zt[2 * ATT_W:3 * ATT_W, skip:]
        lft_ref[0, :, dst] = lf[:, skip:]
        ct_ref[0, :, dst] = run[:, skip:]


def _inproj_t(xp, g, wt, wft, bft, k_all, v_all, layer, tl):
    b, seq_len, d = xp.shape
    tri = (lax.broadcasted_iota(jnp.int32, (tl, tl), 0) <= lax.broadcasted_iota(jnp.int32, (tl, tl), 1)).astype(BF16)
    const = lambda i: (layer, 0, 0)
    seq = lambda i: (i, 0, 0)
    slab = lambda i: (layer, i, 0, 0)
    return pl.pallas_call(
        functools.partial(_inproj_t_body, tl=tl),
        grid=(b,),
        in_specs=[
            pl.BlockSpec((1, seq_len, d), seq),
            pl.BlockSpec((None, 1, d), const),
            pl.BlockSpec((None, 3 * ATT_W, d), const),
            pl.BlockSpec((None, 16, d), const),
            pl.BlockSpec((None, N_HEADS, 1), const),
            pl.BlockSpec((tl, tl), lambda i: (0, 0)),
            pl.BlockSpec(memory_space=pl.ANY),
            pl.BlockSpec(memory_space=pl.ANY),
        ],
        out_specs=[
            pl.BlockSpec((1, ATT_W, seq_len), seq),
            pl.BlockSpec((None, 1, ATT_W, seq_len), slab),
            pl.BlockSpec((None, 1, ATT_W, seq_len), slab),
            pl.BlockSpec((1, N_HEADS, seq_len), seq),
            pl.BlockSpec((1, N_HEADS, seq_len), seq),
        ],
        out_shape=[
            jax.ShapeDtypeStruct((b, ATT_W, seq_len), BF16),
            jax.ShapeDtypeStruct(k_all.shape, F32),
            jax.ShapeDtypeStruct(v_all.shape, F32),
            jax.ShapeDtypeStruct((b, N_HEADS, seq_len), F32),
            jax.ShapeDtypeStruct((b, N_HEADS, seq_len), F32),
        ],
        input_output_aliases={6: 1, 7: 2},
        compiler_params=_cparams("arbitrary"),
        name="inproj_feature_major",
    )(xp, g, wt, wft, bft, tri, k_all, v_all)


def _pool_mix_group(tot, cur, cnt, w_ref, scale_ref, g):
    cs = slice(g * POOL_GROUP_W, (g + 1) * POOL_GROUP_W)
    d = tot / cnt - cur
    return _dot(d.astype(BF16), w_ref[g]) * scale_ref[:, cs]


def _pool_seq_body(x_ref, g_ref, wu_ref, w_ref, scale_ref, o_ref, tail_ref, pad_ref, *, chunk):
    seq_len = x_ref.shape[1]
    pad_ref[0:HIST_ROWS, :] = jnp.zeros((HIST_ROWS, POOL_W), F32)
    for r0 in range(0, seq_len, chunk):
        h = _rms(x_ref[0, r0:r0 + chunk, :], g_ref[...]).astype(BF16)
        pad_ref[HIST_ROWS + r0:HIST_ROWS + r0 + chunk, :] = _dot(h, wu_ref[...])
    tail_ref[0] = pad_ref[HIST_ROWS + seq_len - POOL_HIST:HIST_ROWS + seq_len, :]
    for r0 in range(0, seq_len, chunk):
        pos = r0 + lax.broadcasted_iota(jnp.int32, (chunk, 1), 0)
        for g, w in enumerate(POOL_WINDOWS):
            cs = slice(g * POOL_GROUP_W, (g + 1) * POOL_GROUP_W)
            base = HIST_ROWS + r0
            cur = pad_ref[base:base + chunk, cs]
            tot = cur
            for i in range(1, w):
                tot = tot + pad_ref[base - i:base - i + chunk, cs]
            cnt = jnp.minimum(w, pos + 1).astype(F32)
            o_ref[0, r0:r0 + chunk, cs] = _pool_mix_group(tot, cur, cnt, w_ref, scale_ref, g).astype(o_ref.dtype)


def _pool_seq(x, g, wu, w_grp, scale, layer):
    b, seq_len, d = x.shape
    chunk = seq_len // 3
    const3 = lambda i: (layer, 0, 0)
    return pl.pallas_call(
        functools.partial(_pool_seq_body, chunk=chunk),
        grid=(b,),
        in_specs=[
            pl.BlockSpec((1, seq_len, d), lambda i: (i, 0, 0)),
            pl.BlockSpec((None, 1, d), const3),
            pl.BlockSpec((None, d, POOL_W), const3),
            pl.BlockSpec((None, len(POOL_WINDOWS), POOL_GROUP_W, POOL_GROUP_W), lambda i: (layer, 0, 0, 0)),
            pl.BlockSpec((None, 1, POOL_W), const3),
        ],
        out_specs=[pl.BlockSpec((1, seq_len, POOL_W), lambda i: (i, 0, 0)),
                   pl.BlockSpec((1, POOL_HIST, POOL_W), lambda i: (i, 0, 0))],
        out_shape=[jax.ShapeDtypeStruct((b, seq_len, POOL_W), BF16),
                   jax.ShapeDtypeStruct((b, POOL_HIST, POOL_W), F32)],
        scratch_shapes=[pltpu.VMEM((HIST_ROWS + seq_len, POOL_W), F32)],
        compiler_params=_cparams("arbitrary"),
        name="pool_prompt",
    )(x, g, wu, w_grp, scale)


def _pool_step_body(hist_ref, u_ref, w_ref, scale_ref, o_ref, *, pos0):
    n_hist = hist_ref.shape[1]
    t_new = u_ref.shape[0]

    def slab(i):
        return hist_ref[0, i] if i < n_hist else u_ref[i - n_hist]

    for t in range(t_new):
        end = n_hist + t
        for g, w in enumerate(POOL_WINDOWS):
            cs = slice(g * POOL_GROUP_W, (g + 1) * POOL_GROUP_W)
            cur = slab(end)[:, cs]
            tot = cur
            for i in range(1, w):
                if end - i >= 0:
                    tot = tot + slab(end - i)[:, cs]
            cnt = float(min(w, pos0 + t + 1))
            o_ref[t, :, cs] = _pool_mix_group(tot, cur, cnt, w_ref, scale_ref, g).astype(o_ref.dtype)


def _pool_step(hist, u_new, w_grp, scale, layer, pos0):
    _, n_hist, nb, _ = hist.shape
    t_new = u_new.shape[0]
    return pl.pallas_call(
        functools.partial(_pool_step_body, pos0=pos0),
        grid=(1,),
        in_specs=[
            pl.BlockSpec((1, n_hist, nb, POOL_W), lambda i: (layer, 0, 0, 0)),
            pl.BlockSpec((t_new, nb, POOL_W), lambda i: (0, 0, 0)),
            pl.BlockSpec((None, len(POOL_WINDOWS), POOL_GROUP_W, POOL_GROUP_W), lambda i: (layer, 0, 0, 0)),
            pl.BlockSpec((None, 1, POOL_W), lambda i: (layer, 0, 0)),
        ],
        out_specs=pl.BlockSpec((t_new, nb, POOL_W), lambda i: (0, 0, 0)),
        out_shape=jax.ShapeDtypeStruct((t_new, nb, POOL_W), F32),
        compiler_params=_cparams("arbitrary"),
        name="pool_sample",
    )(hist, u_new, w_grp, scale)


def _attn_body(qt_ref, kt_ref, vt_ref, ct_ref, o_ref, qa_ref, ka_ref, va_ref, kpad_ref, *, blk, q_lanes):
    hp = pl.program_id(1)
    seq_len = qt_ref.shape[2]
    nblk = ka_ref.shape[1]
    q_pad = qa_ref.shape[2]

    row = lax.broadcasted_iota(jnp.int32, (LANES, seq_len), 0)
    qpair = qt_ref[0].astype(F32)
    kpair = kt_ref[0]
    vpair = vt_ref[0].astype(BF16)
    n_full = seq_len // blk
    tail = seq_len - n_full * blk
    assert nblk == n_full + 1 and 0 < tail

    for j in range(n_full):
        va_ref[j] = vpair[:, j * blk:(j + 1) * blk]
    va_ref[n_full] = jnp.zeros((LANES, blk), BF16)
    va_ref[n_full, :, 0:tail] = vpair[:, n_full * blk:seq_len]

    kpad_ref[:, n_full * blk:nblk * blk] = jnp.zeros((LANES, blk), F32)
    qa_ref[...] = jnp.zeros(qa_ref.shape, BF16)
    for hh in range(2):
        f0 = HEAD_DIM * (1 - hh)
        c1, c2, c3 = _split3(ct_ref[0, pl.ds(2 * hp + hh, 1), :])
        in_head = (row >= HEAD_DIM * hh) & (row < HEAD_DIM * (hh + 1))
        qa = jnp.where(in_head, qpair,
             jnp.where(row == f0, c1, jnp.where(row == f0 + 1, c2, jnp.where(row == f0 + 2, c3,
             jnp.where((row >= f0 + 3) & (row < f0 + 6), 1.0, 0.0)))))
        ka = jnp.where(in_head, kpair,
             jnp.where((row >= f0) & (row < f0 + 3), 1.0,
             jnp.where(row == f0 + 3, -c1, jnp.where(row == f0 + 4, -c2, jnp.where(row == f0 + 5, -c3, 0.0)))))
        qa_ref[hh, :, 0:seq_len] = qa.astype(BF16)
        kpad_ref[:, 0:seq_len] = ka
        for j in range(nblk):
            ka_ref[hh, j] = jnp.transpose(kpad_ref[:, j * blk:(j + 1) * blk]).astype(BF16)

    tail_w = pl.cdiv(tail, LANES) * LANES

    def update(state, s, j, hh):
        m, l, acc = state
        m_new = jnp.maximum(m, jnp.max(s, axis=0, keepdims=True))
        alpha = jnp.exp(m - m_new)
        p = jnp.exp(s - m_new)
        l = alpha * l + jnp.sum(p, axis=0, keepdims=True)
        acc = alpha * acc + _dot(va_ref[j, HEAD_DIM * hh:HEAD_DIM * (hh + 1), 0:s.shape[0]], p.astype(BF16))
        return m_new, l, acc

    n_q = max(q_pad // q_lanes, 1)
    for i in range(n_q):
        q0 = i * q_lanes
        q1 = q_pad if i == n_q - 1 else q0 + q_lanes
        r1 = min(q1, seq_len)
        n = q1 - q0
        qb = [qa_ref[hh, :, q0:q1] for hh in range(2)]
        n_open = q0 // blk
        n_need = pl.cdiv(r1, blk)

        def open_block(j, states, qb=qb):
            return tuple(update(states[hh], _dot(ka_ref[hh, j], qb[hh]), j, hh) for hh in range(2))

        init = (jnp.full((1, n), NEG, F32), jnp.zeros((1, n), F32), jnp.zeros((HEAD_DIM, n), F32))
        states = lax.fori_loop(0, n_open, open_block, (init, init), unroll=True)
        for j in range(n_open, n_need):
            kb = blk if j < n_full else tail_w
            key = j * blk + lax.broadcasted_iota(jnp.int32, (kb, n), 0)
            query = q0 + lax.broadcasted_iota(jnp.int32, (kb, n), 1)
            states = tuple(update(states[hh], jnp.where(key <= query, _dot(ka_ref[hh, j, 0:kb, :], qb[hh]), NEG),
                                  j, hh) for hh in range(2))
        (_, l0, acc0), (_, l1, acc1) = states
        out = jnp.transpose(jnp.concatenate([acc0 / l0, acc1 / l1], axis=0))
        o_ref[0, q0:r1, :] = out[0:r1 - q0, :].astype(o_ref.dtype)


def _attn_prompt(qt, k_all, v_all, ct, layer):
    b, _, seq_len = qt.shape
    blk = ATT_BLOCK
    nblk = pl.cdiv(seq_len, blk)
    q_pad = pl.cdiv(seq_len, LANES) * LANES
    pair = lambda i, p: (i, p, 0)
    pair_l = lambda i, p: (layer, i, p, 0)
    return pl.pallas_call(
        functools.partial(_attn_body, blk=blk, q_lanes=ATT_Q_ROWS),
        grid=(b, N_HEADS // 2),
        in_specs=[
            pl.BlockSpec((1, LANES, seq_len), pair),
            pl.BlockSpec((None, 1, LANES, seq_len), pair_l),
            pl.BlockSpec((None, 1, LANES, seq_len), pair_l),
            pl.BlockSpec((1, N_HEADS, seq_len), lambda i, p: (i, 0, 0)),
        ],
        out_specs=pl.BlockSpec((1, seq_len, LANES), lambda i, p: (i, 0, p)),
        out_shape=jax.ShapeDtypeStruct((b, seq_len, ATT_W), BF16),
        scratch_shapes=[
            pltpu.VMEM((2, LANES, q_pad), BF16),
            pltpu.VMEM((2, nblk, blk, LANES), BF16),
            pltpu.VMEM((nblk, LANES, blk), BF16),
            pltpu.VMEM((LANES, nblk * blk), F32),
        ],
        compiler_params=_cparams("arbitrary", "arbitrary"),
        name="attn_prompt",
    )(qt, k_all, v_all, ct)


def _decode_body(pt_ref, q_ref, kn_ref, vn_ref, lfn_ref, *rest, n_pg, t_new):
    k_refs = rest[0:n_pg]
    v_refs = rest[n_pg:2 * n_pg]
    f_refs = rest[2 * n_pg:3 * n_pg]
    o_ref = rest[3 * n_pg]
    qbd_ref, m_ref, l_ref, acc_ref, carry_ref, cnew_ref = rest[3 * n_pg + 1:]
    j = pl.program_id(1)
    n_rows = t_new * N_HEADS

    row_h = lax.broadcasted_iota(jnp.int32, (n_rows, ATT_W), 0) & (N_HEADS - 1)
    own = (lax.broadcasted_iota(jnp.int32, (n_rows, ATT_W), 1) >> HEAD_SHIFT) == row_h

    r_i = lax.broadcasted_iota(jnp.int32, (n_rows, n_rows), 0)
    c_i = lax.broadcasted_iota(jnp.int32, (n_rows, n_rows), 1)
    same_head = (r_i & (N_HEADS - 1)) == (c_i & (N_HEADS - 1))

    def new_token_sum(lo):
        sel = same_head & ((c_i >> TOK_SHIFT) > lo) & ((c_i >> TOK_SHIFT) <= (r_i >> TOK_SHIFT))
        return jnp.sum(jnp.where(sel, lfn_ref[0], 0.0), axis=1, keepdims=True)

    @pl.when(j == 0)
    def _():
        q4 = q_ref[0]
        rep = jnp.concatenate([jnp.broadcast_to(q4[t:t + 1], (N_HEADS, ATT_W)) for t in range(t_new)], axis=0)
        qbd_ref[...] = jnp.where(own, rep, 0.0).astype(BF16)
        m_ref[...] = jnp.full_like(m_ref, NEG)
        l_ref[...] = jnp.zeros_like(l_ref)
        acc_ref[...] = jnp.zeros_like(acc_ref)
        carry_ref[...] = jnp.zeros_like(carry_ref)
        cnew_ref[...] = new_token_sum(-1)

    def update(s, pv):
        m = m_ref[...]
        m_new = jnp.maximum(m, jnp.max(s, axis=1, keepdims=True))
        alpha = jnp.exp(m - m_new)
        p = jnp.exp(s - m_new)
        l_ref[...] = alpha * l_ref[...] + jnp.sum(p, axis=1, keepdims=True)
        p = p.astype(BF16)
        out = pv(p[:, 0:PAGE], 0)
        for k in range(1, s.shape[1] // PAGE):
            out = out + pv(p[:, k * PAGE:(k + 1) * PAGE], k)
        acc_ref[...] = alpha * acc_ref[...] + out
        m_ref[...] = m_new

    later = (lax.broadcasted_iota(jnp.int32, (PAGE, PAGE), 0) >
             lax.broadcasted_iota(jnp.int32, (PAGE, PAGE), 1)).astype(BF16)
    qbd = qbd_ref[...]
    f_all = jnp.concatenate([f_refs[p][...] for p in range(n_pg)], axis=0)
    a, b, c = _split3(f_all)
    cs = _dot(jnp.concatenate([a, b, c], axis=0).astype(BF16), later)
    nf = n_pg * N_HEADS
    within = cs[0:nf] + cs[nf:2 * nf] + cs[2 * nf:3 * nf]
    total = jnp.sum(f_all, axis=1, keepdims=True)
    carry = carry_ref[...]
    tiles = []
    for p in range(n_pg):
        rows = slice(p * N_HEADS, (p + 1) * N_HEADS)
        suffix = within[rows] + carry
        carry = carry + total[rows]
        kt = k_refs[p][...].reshape(ATT_W, PAGE).astype(BF16)
        tiles.append(_dot(qbd, kt) + jnp.concatenate([suffix] * t_new, axis=0))
    carry_ref[...] = carry
    s = jnp.concatenate(tiles, axis=1) + cnew_ref[...]
    update(s, lambda pb, p: _dot_nt(pb, v_refs[p][...].reshape(ATT_W, PAGE).astype(BF16)))

    @pl.when(j == pl.num_programs(1) - 1)
    def _():
        zpad = jnp.zeros((PAGE - t_new, ATT_W), F32)
        kpad = jnp.concatenate([kn_ref[0], zpad], axis=0).astype(BF16)
        vpad = jnp.concatenate([vn_ref[0], zpad], axis=0).astype(BF16)
        lane = lax.broadcasted_iota(jnp.int32, (n_rows, PAGE), 1)
        tok = lax.broadcasted_iota(jnp.int32, (n_rows, PAGE), 0) >> TOK_SHIFT
        bias = jnp.zeros((n_rows, PAGE), F32)
        for t in range(t_new):
            bias = jnp.where(lane == t, new_token_sum(t), bias)
        s = jnp.where(lane <= tok, _dot_nt(qbd, kpad) + bias, NEG)
        update(s, lambda pb, k: _dot(pb, vpad))
        out = jnp.where(own, acc_ref[...] / l_ref[...], 0.0)
        o_ref[0] = jnp.sum(out.reshape(t_new, N_HEADS, ATT_W), axis=1)


def _decode(page_table, q, k_new, v_new, lf_new, ckt, cvt, cft, layer):
    nb, t_new, _ = q.shape
    n_pages = page_table.shape[1]
    n_pg = DEC_PAGES_PER_STEP
    n_rows = t_new * N_HEADS

    def page_spec(p, shape):
        zeros = (0,) * len(shape)

        def index(b, j, pt):
            return (layer, pt[b, n_pages - 1 - (j * n_pg + p)]) + zeros

        return pl.BlockSpec((None, None) + shape, index)

    new = lambda b, j, pt: (b, 0, 0)
    in_specs = [pl.BlockSpec((1, t_new, ATT_W), new), pl.BlockSpec((1, t_new, ATT_W), new),
                pl.BlockSpec((1, t_new, ATT_W), new), pl.BlockSpec((1, 1, n_rows), new)]
    in_specs += [page_spec(p, (N_HEADS, HEAD_DIM, PAGE)) for p in range(n_pg)]
    in_specs += [page_spec(p, (N_HEADS, HEAD_DIM, PAGE)) for p in range(n_pg)]
    in_specs += [page_spec(p, (N_HEADS, PAGE)) for p in range(n_pg)]
    grid_spec = pltpu.PrefetchScalarGridSpec(
        num_scalar_prefetch=1,
        grid=(nb, n_pages // n_pg),
        in_specs=in_specs,
        out_specs=pl.BlockSpec((1, t_new, ATT_W), new),
        scratch_shapes=[
            pltpu.VMEM((n_rows, ATT_W), BF16),
            pltpu.VMEM((n_rows, 1), F32),
            pltpu.VMEM((n_rows, 1), F32),
            pltpu.VMEM((n_rows, ATT_W), F32),
            pltpu.VMEM((N_HEADS, 1), F32),
            pltpu.VMEM((n_rows, 1), F32),
        ],
    )
    return pl.pallas_call(
        functools.partial(_decode_body, n_pg=n_pg, t_new=t_new),
        grid_spec=grid_spec,
        out_shape=jax.ShapeDtypeStruct((nb, t_new, ATT_W), F32),
        compiler_params=_cparams("arbitrary", "arbitrary"),
        name="attn_decode",
    )(page_table, q, k_new, v_new, lf_new, *([ckt] * n_pg), *([cvt] * n_pg), *([cft] * n_pg))


def _merge_body(x_ref, pu_ref, ao_ref, gpre_ref, wg_ref, wpu_ref, wau_ref, wout_ref, gpost_ref, o_ref, m_ref,
                *, chunk):
    x = x_ref[...]
    d = x.shape[1]
    h = _rms(x, gpre_ref[...]).astype(BF16)
    pu = pu_ref[...].astype(BF16)
    ao = ao_ref[...].astype(BF16)
    for c in range(d // chunk):
        sl = slice(c * chunk, (c + 1) * chunk)
        sl_att = slice(d + c * chunk, d + (c + 1) * chunk)
        m = (jax.nn.sigmoid(_dot(h, wg_ref[:, sl])) * _dot(pu, wpu_ref[:, sl])
             + jax.nn.sigmoid(_dot(h, wg_ref[:, sl_att])) * _dot(ao, wau_ref[:, sl]))
        m_ref[:, sl] = m.astype(BF16)
    y = _dot(m_ref[...], wout_ref[...])
    o_ref[...] = x + _rms(y, gpost_ref[...])


def _merge(x, pu, ao, g_pre, wg, wpu, wau, wout, g_post, layer, tm):
    n, d = x.shape
    row = lambda i: (i, 0)
    const = lambda i: (layer, 0, 0)
    return pl.pallas_call(
        functools.partial(_merge_body, chunk=256),
        grid=(n // tm,),
        in_specs=[
            pl.BlockSpec((tm, d), row),
            pl.BlockSpec((tm, POOL_W), row),
            pl.BlockSpec((tm, ATT_W), row),
            pl.BlockSpec((None, 1, d), const),
            pl.BlockSpec((None, d, 2 * d), const),
            pl.BlockSpec((None, POOL_W, d), const),
            pl.BlockSpec((None, ATT_W, d), const),
            pl.BlockSpec((None, d, d), const),
            pl.BlockSpec((None, 1, d), const),
        ],
        out_specs=pl.BlockSpec((tm, d), row),
        out_shape=jax.ShapeDtypeStruct((n, d), F32),
        scratch_shapes=[pltpu.VMEM((tm, d), BF16)],
        compiler_params=_cparams("arbitrary"),
        name="merge",
    )(x, pu, ao, g_pre, wg, wpu, wau, wout, g_post)


def _row_tile(n, cap):
    best = None
    for t in range(16, cap + 1, 16):
        if n % t == 0:
            best = t
    assert best is not None, n
    return best


def kernel(x_prompt, x_sample, cache_k, cache_v, cache_logf, state_pool, page_table, meta_tokens,
           ln_ffn1_pre, ln_ffn1_post, ffn1_wi, ffn1_wo, ln_mix_pre, ln_mix_post, w_in, b_forget,
           pool_w_grp, pool_scale, w_pool_up, w_att_up, w_out, ln_ffn2_pre, ln_ffn2_post,
           ffn2_wi, ffn2_wo):
    nbp, seq, d = x_prompt.shape
    nbs, t_new, _ = x_sample.shape
    depth = w_in.shape[0]
    n_pages = page_table.shape[1]
    past_len = n_pages * PAGE
    seq_len = seq + N_META

    meta = jnp.broadcast_to(meta_tokens.astype(x_prompt.dtype)[None], (nbp, N_META, d))
    xp = jnp.concatenate([meta, x_prompt], axis=1).reshape(nbp * seq_len, d)
    xs = x_sample.reshape(nbs * t_new, d)
    tm_p = _row_tile(nbp * seq_len, 768)
    tm_s = nbs * t_new

    bf = lambda w: w.astype(BF16)
    vec = lambda g: g[:, None, :]
    o1, o2, o3, o4, o5 = POOL_W, POOL_W + ATT_W, POOL_W + 2 * ATT_W, POOL_W + 3 * ATT_W, POOL_W + 3 * ATT_W + N_HEADS
    w_in_b = bf(w_in)
    w_f_pad = jnp.pad(w_in_b[:, :, o4:o5], ((0, 0), (0, 0), (0, LANES - N_HEADS)))
    w_u_b = w_in_b[:, :, 0:o1]
    w_rows_s = jnp.concatenate([w_in_b[:, :, 0:o4], w_f_pad], axis=2)
    w_gate_b = w_in_b[:, :, o5:]
    w_qkv_t = jnp.transpose(w_in_b[:, :, o1:o4], (0, 2, 1))
    w_f_t = jnp.pad(jnp.transpose(w_in_b[:, :, o4:o5], (0, 2, 1)), ((0, 0), (0, 16 - N_HEADS), (0, 0)))
    bf_row = jnp.pad(b_forget, ((0, 0), (0, LANES - N_HEADS)))[:, None, :]
    bf_col = b_forget[:, :, None]
    ffn1_wi_b, ffn1_wo_b, ffn2_wi_b, ffn2_wo_b = bf(ffn1_wi), bf(ffn1_wo), bf(ffn2_wi), bf(ffn2_wo)
    w_grp_b, wpu_b, wau_b, wout_b = bf(pool_w_grp), bf(w_pool_up), bf(w_att_up), bf(w_out)
    scale3 = vec(pool_scale)
    g1pre, g1post, gmpre, gmpost, g2pre, g2post = (vec(g) for g in (
        ln_ffn1_pre, ln_ffn1_post, ln_mix_pre, ln_mix_post, ln_ffn2_pre, ln_ffn2_post))

    ckt = jnp.transpose(cache_k, (0, 1, 3, 4, 2))
    cvt = jnp.transpose(cache_v, (0, 1, 3, 4, 2))
    cft = jnp.transpose(cache_logf, (0, 1, 3, 2))
    hist = jnp.transpose(state_pool, (0, 2, 1, 3))

    k_all = jnp.zeros((depth, nbp, ATT_W, seq_len), F32)
    v_all = jnp.zeros((depth, nbp, ATT_W, seq_len), F32)
    fp_l, pp_l, ks_l, vs_l, fs_l, ps_l = [], [], [], [], [], []
    for l in range(depth):
        xp = _ffn(xp, g1pre, g1post, ffn1_wi_b, ffn1_wo_b, l, tm_p)
        xs = _ffn(xs, g1pre, g1post, ffn1_wi_b, ffn1_wo_b, l, tm_s)

        xp3 = xp.reshape(nbp, seq_len, d)
        qt, k_all, v_all, lft, ct = _inproj_t(xp3, gmpre, w_qkv_t, w_f_t, bf_col, k_all, v_all, l, KV_TILE)
        pu, u_tail = _pool_seq(xp3, gmpre, w_u_b, w_grp_b, scale3, l)
        ao = _attn_prompt(qt, k_all, v_all, ct, l)
        xp = _merge(xp, pu.reshape(-1, POOL_W), ao.reshape(-1, ATT_W), gmpre, w_gate_b, wpu_b, wau_b, wout_b,
                    gmpost, l, tm_p)
        fp_l.append(lft); pp_l.append(u_tail)

        u, q, k, v, lf = _inproj_rows(xs, gmpre, w_rows_s, bf_row, l, tm_s)
        u_tb = jnp.transpose(u.reshape(nbs, t_new, POOL_W), (1, 0, 2))
        pu = jnp.transpose(_pool_step(hist, u_tb, w_grp_b, scale3, l, past_len), (1, 0, 2))
        lf8 = lf[:, 0:N_HEADS].reshape(nbs, t_new, N_HEADS)
        ao = _decode(page_table, q.reshape(nbs, t_new, ATT_W), k.reshape(nbs, t_new, ATT_W),
                     v.reshape(nbs, t_new, ATT_W), lf8.reshape(nbs, 1, t_new * N_HEADS), ckt, cvt, cft, l)
        xs = _merge(xs, pu.reshape(-1, POOL_W), ao.reshape(-1, ATT_W), gmpre, w_gate_b, wpu_b, wau_b, wout_b,
                    gmpost, l, tm_s)
        ks_l.append(k.reshape(nbs, t_new, N_HEADS, HEAD_DIM)); vs_l.append(v.reshape(nbs, t_new, N_HEADS, HEAD_DIM))
        fs_l.append(lf8)
        ps_l.append(jnp.concatenate([state_pool[l], u.reshape(nbs, t_new, POOL_W)], axis=1)[:, t_new:])

        xp = _ffn(xp, g2pre, g2post, ffn2_wi_b, ffn2_wo_b, l, tm_p)
        xs = _ffn(xs, g2pre, g2post, ffn2_wi_b, ffn2_wo_b, l, tm_s)

    y_prompt = xp.reshape(nbp, seq_len, d)[:, N_META:]
    y_sample = xs.reshape(nbs, t_new, d)
    heads_t = lambda a: jnp.transpose(a.reshape(depth, nbp, N_HEADS, HEAD_DIM, seq_len), (0, 1, 4, 2, 3))
    return (y_prompt, y_sample,
            heads_t(k_all), heads_t(v_all), jnp.transpose(jnp.stack(fp_l), (0, 1, 3, 2)), jnp.stack(pp_l),
            jnp.stack(ks_l), jnp.stack(vs_l), jnp.stack(fs_l), jnp.stack(ps_l))
```

```python
import functools

import jax
import jax.numpy as jnp
from jax import lax
from jax.experimental import pallas as pl
from jax.experimental.pallas import tpu as pltpu

F32 = jnp.float32
BF16 = jnp.bfloat16

RMS_EPS = 1e-6
N_META = 16
N_HEADS = 8
HEAD_DIM = 64
HEAD_SHIFT = 6
TOK_SHIFT = 3
ATT_W = N_HEADS * HEAD_DIM
POOL_WINDOWS = (2, 4, 8, 16)
POOL_GROUP_W = 128
POOL_W = len(POOL_WINDOWS) * POOL_GROUP_W
POOL_HIST = max(POOL_WINDOWS) - 1
HIST_ROWS = 16
PAGE = 128
LANES = 128
NEG = float(jnp.finfo(jnp.float32).min)
LOG2E = 1.4426950408889634

ATT_BLOCK = 512
ATT_Q_ROWS = 512
KV_TILE = 512
DEC_PAGES_PER_STEP = 32
VMEM_LIMIT = 56 * 1024 * 1024


def _cparams(*sem):
    return pltpu.CompilerParams(dimension_semantics=sem, vmem_limit_bytes=VMEM_LIMIT)


def _rms(x, g):
    ms = jnp.mean(x * x, axis=-1, keepdims=True)
    return x * lax.rsqrt(ms + RMS_EPS) * g


def _log_sigmoid(x):
    return -(jnp.maximum(-x, 0.0) + jnp.log1p(jnp.exp(-jnp.abs(x))))


def _split3(x):
    a = x.astype(BF16).astype(F32)
    r = x - a
    b = r.astype(BF16).astype(F32)
    c = (r - b).astype(BF16).astype(F32)
    return a, b, c


def _dot(a, b):
    return jnp.dot(a, b, preferred_element_type=F32)


def _dot_nt(a, b):
    return lax.dot_general(a, b, (((1,), (1,)), ((), ())), preferred_element_type=F32)


def _ffn_body(x_ref, gpre_ref, gpost_ref, wig_ref, wiu_ref, wo_ref, o_ref, a_ref, *, chunk):
    x = x_ref[...]
    h = _rms(x, gpre_ref[...]).astype(BF16)
    d_ff = a_ref.shape[1]
    for c in range(d_ff // chunk):
        sl = slice(c * chunk, (c + 1) * chunk)
        g = _dot(h, wig_ref[:, sl])
        u = _dot(h, wiu_ref[:, sl])
        a_ref[:, sl] = (g * jax.nn.sigmoid(g) * u).astype(BF16)
    y = _dot(a_ref[...], wo_ref[...])
    o_ref[...] = x + 0.5 * _rms(y, gpost_ref[...])


def _ffn(x, g_pre, g_post, wi, wo, layer, tm):
    n, d = x.shape
    d_ff = wo.shape[1]
    const = lambda i: (layer, 0, 0)
    return pl.pallas_call(
        functools.partial(_ffn_body, chunk=256),
        grid=(n // tm,),
        in_specs=[
            pl.BlockSpec((tm, d), lambda i: (i, 0)),
            pl.BlockSpec((None, 1, d), const),
            pl.BlockSpec((None, 1, d), const),
            pl.BlockSpec((None, d, d_ff), const),
            pl.BlockSpec((None, d, d_ff), lambda i: (layer, 0, 1)),
            pl.BlockSpec((None, d_ff, d), const),
        ],
        out_specs=pl.BlockSpec((tm, d), lambda i: (i, 0)),
        out_shape=jax.ShapeDtypeStruct((n, d), F32),
        scratch_shapes=[pltpu.VMEM((tm, d_ff), BF16)],
        compiler_params=_cparams("arbitrary"),
        name="ffn_half_step",
    )(x, g_pre, g_post, wi, wi, wo)


def _inproj_rows_body(x_ref, g_ref, w_ref, bf_ref, u_ref, q_ref, k_ref, v_ref, lf_ref):
    h = _rms(x_ref[...], g_ref[...]).astype(BF16)
    u_ref[...] = _dot(h, w_ref[:, 0:POOL_W])
    q_ref[...] = _dot(h, w_ref[:, POOL_W:POOL_W + ATT_W]) * (HEAD_DIM ** -0.5)
    k0 = POOL_W + ATT_W
    k_ref[...] = _dot(h, w_ref[:, k0:k0 + ATT_W])
    v_ref[...] = _dot(h, w_ref[:, k0 + ATT_W:k0 + 2 * ATT_W])
    zf = _dot(h, w_ref[:, k0 + 2 * ATT_W:k0 + 2 * ATT_W + LANES])
    lf_ref[...] = _log_sigmoid(zf + bf_ref[...])


def _inproj_rows(x, g, w, bf, layer, tm):
    n, d = x.shape
    wcols = w.shape[2]
    const = lambda i: (layer, 0, 0)
    row = lambda i: (i, 0)
    widths = (POOL_W, ATT_W, ATT_W, ATT_W, LANES)
    return pl.pallas_call(
        _inproj_rows_body,
        grid=(n // tm,),
        in_specs=[
            pl.BlockSpec((tm, d), row),
            pl.BlockSpec((None, 1, d), const),
            pl.BlockSpec((None, d, wcols), const),
            pl.BlockSpec((None, 1, LANES), const),
        ],
        out_specs=[pl.BlockSpec((tm, w_), row) for w_ in widths],
        out_shape=[jax.ShapeDtypeStruct((n, w_), F32) for w_ in widths],
        compiler_params=_cparams("arbitrary"),
        name="inproj_rows",
    )(x, g, w, bf)


def _inproj_t_body(x_ref, g_ref, wt_ref, wft_ref, bft_ref, tri_ref, kall_ref, vall_ref,
                   qt_ref, kt_ref, vt_ref, lft_ref, ct_ref, *, tl):
    del kall_ref, vall_ref
    seq_len = x_ref.shape[1]
    n_full = seq_len // tl
    rem = seq_len - n_full * tl
    win = pl.cdiv(rem, LANES) * LANES
    assert win <= tl
    carry = jnp.zeros((N_HEADS, 1), F32)
    chunks = [(c * tl, tl, 0) for c in range(n_full)] + ([(seq_len - win, win, win - rem)] if rem else [])
    for p0, width, skip in chunks:
        h = _rms(x_ref[0, p0:p0 + width, :], g_ref[...]).astype(BF16)
        zt = _dot_nt(wt_ref[...], h)
        lf = _log_sigmoid(_dot_nt(wft_ref[...], h)[0:N_HEADS] + bft_ref[...])
        fresh = lax.broadcasted_iota(jnp.int32, lf.shape, 1) >= skip
        a, b, c = _split3(jnp.where(fresh, lf, 0.0))
        stack = jnp.concatenate([a, b, c, jnp.zeros_like(a)], axis=0).astype(BF16)
        cs = _dot(stack, tri_ref[0:width, 0:width])
        run = cs[0:8] + cs[8:16] + cs[16:24] + carry
        carry = run[:, width - 1:width]
        dst = slice(p0 + skip, p0 + width)
        qt_ref[0, :, dst] = (zt[0:ATT_W, skip:] * (LOG2E * HEAD_DIM ** -0.5)).astype(qt_ref.dtype)
        kt_ref[0, :, dst] = zt[ATT_W:2 * ATT_W, skip:]
        vt_ref[0, :, dst] = zt[2 * ATT_W:3 * ATT_W, skip:]
        lft_ref[0, :, dst] = lf[:, skip:]
        ct_ref[0, :, dst] = run[:, skip:]


def _inproj_t(xp, g, wt, wft, bft, k_all, v_all, layer, tl):
    b, seq_len, d = xp.shape
    tri = (lax.broadcasted_iota(jnp.int32, (tl, tl), 0) <= lax.broadcasted_iota(jnp.int32, (tl, tl), 1)).astype(BF16)
    const = lambda i: (layer, 0, 0)
    seq = lambda i: (i, 0, 0)
    slab = lambda i: (layer, i, 0, 0)
    return pl.pallas_call(
        functools.partial(_inproj_t_body, tl=tl),
        grid=(b,),
        in_specs=[
            pl.BlockSpec((1, seq_len, d), seq),
            pl.BlockSpec((None, 1, d), const),
            pl.BlockSpec((None, 3 * ATT_W, d), const),
            pl.BlockSpec((None, 16, d), const),
            pl.BlockSpec((None, N_HEADS, 1), const),
            pl.BlockSpec((tl, tl), lambda i: (0, 0)),
            pl.BlockSpec(memory_space=pl.ANY),
            pl.BlockSpec(memory_space=pl.ANY),
        ],
        out_specs=[
            pl.BlockSpec((1, ATT_W, seq_len), seq),
            pl.BlockSpec((None, 1, ATT_W, seq_len), slab),
            pl.BlockSpec((None, 1, ATT_W, seq_len), slab),
            pl.BlockSpec((1, N_HEADS, seq_len), seq),
            pl.BlockSpec((1, N_HEADS, seq_len), seq),
        ],
        out_shape=[
            jax.ShapeDtypeStruct((b, ATT_W, seq_len), BF16),
            jax.ShapeDtypeStruct(k_all.shape, F32),
            jax.ShapeDtypeStruct(v_all.shape, F32),
            jax.ShapeDtypeStruct((b, N_HEADS, seq_len), F32),
            jax.ShapeDtypeStruct((b, N_HEADS, seq_len), F32),
        ],
        input_output_aliases={6: 1, 7: 2},
        compiler_params=_cparams("arbitrary"),
        name="inproj_feature_major",
    )(xp, g, wt, wft, bft, tri, k_all, v_all)


def _pool_mix_group(tot, cur, cnt, w_ref, scale_ref, g):
    cs = slice(g * POOL_GROUP_W, (g + 1) * POOL_GROUP_W)
    d = tot / cnt - cur
    return _dot(d.astype(BF16), w_ref[g]) * scale_ref[:, cs]


def _pool_seq_body(x_ref, g_ref, wu_ref, w_ref, scale_ref, o_ref, tail_ref, pad_ref, *, chunk):
    seq_len = x_ref.shape[1]
    pad_ref[0:HIST_ROWS, :] = jnp.zeros((HIST_ROWS, POOL_W), F32)
    for r0 in range(0, seq_len, chunk):
        h = _rms(x_ref[0, r0:r0 + chunk, :], g_ref[...]).astype(BF16)
        pad_ref[HIST_ROWS + r0:HIST_ROWS + r0 + chunk, :] = _dot(h, wu_ref[...])
    tail_ref[0] = pad_ref[HIST_ROWS + seq_len - POOL_HIST:HIST_ROWS + seq_len, :]
    for r0 in range(0, seq_len, chunk):
        pos = r0 + lax.broadcasted_iota(jnp.int32, (chunk, 1), 0)
        for g, w in enumerate(POOL_WINDOWS):
            cs = slice(g * POOL_GROUP_W, (g + 1) * POOL_GROUP_W)
            base = HIST_ROWS + r0
            cur = pad_ref[base:base + chunk, cs]
            tot = pad_ref[base - HIST_ROWS:base + chunk, cs]
            sh = 1
            while sh < w:
                tot = tot[sh:, :] + tot[:-sh, :]
                sh *= 2
            tot = tot[HIST_ROWS - (w - 1):, :]
            cnt = jnp.minimum(w, pos + 1).astype(F32)
            o_ref[0, r0:r0 + chunk, cs] = _pool_mix_group(tot, cur, cnt, w_ref, scale_ref, g).astype(o_ref.dtype)


def _pool_seq(x, g, wu, w_grp, scale, layer):
    b, seq_len, d = x.shape
    chunk = seq_len // 3
    const3 = lambda i: (layer, 0, 0)
    return pl.pallas_call(
        functools.partial(_pool_seq_body, chunk=chunk),
        grid=(b,),
        in_specs=[
            pl.BlockSpec((1, seq_len, d), lambda i: (i, 0, 0)),
            pl.BlockSpec((None, 1, d), const3),
            pl.BlockSpec((None, d, POOL_W), const3),
            pl.BlockSpec((None, len(POOL_WINDOWS), POOL_GROUP_W, POOL_GROUP_W), lambda i: (layer, 0, 0, 0)),
            pl.BlockSpec((None, 1, POOL_W), const3),
        ],
        out_specs=[pl.BlockSpec((1, seq_len, POOL_W), lambda i: (i, 0, 0)),
                   pl.BlockSpec((1, POOL_HIST, POOL_W), lambda i: (i, 0, 0))],
        out_shape=[jax.ShapeDtypeStruct((b, seq_len, POOL_W), BF16),
                   jax.ShapeDtypeStruct((b, POOL_HIST, POOL_W), F32)],
        scratch_shapes=[pltpu.VMEM((HIST_ROWS + seq_len, POOL_W), F32)],
        compiler_params=_cparams("arbitrary"),
        name="pool_prompt",
    )(x, g, wu, w_grp, scale)


def _pool_step_body(hist_ref, u_ref, w_ref, scale_ref, o_ref, *, pos0):
    n_hist = hist_ref.shape[1]
    t_new = u_ref.shape[0]

    def slab(i):
        return hist_ref[0, i] if i < n_hist else u_ref[i - n_hist]

    for t in range(t_new):
        end = n_hist + t
        for g, w in enumerate(POOL_WINDOWS):
            cs = slice(g * POOL_GROUP_W, (g + 1) * POOL_GROUP_W)
            cur = slab(end)[:, cs]
            tot = cur
            for i in range(1, w):
                if end - i >= 0:
                    tot = tot + slab(end - i)[:, cs]
            cnt = float(min(w, pos0 + t + 1))
            o_ref[t, :, cs] = _pool_mix_group(tot, cur, cnt, w_ref, scale_ref, g).astype(o_ref.dtype)


def _pool_step(hist, u_new, w_grp, scale, layer, pos0):
    _, n_hist, nb, _ = hist.shape
    t_new = u_new.shape[0]
    return pl.pallas_call(
        functools.partial(_pool_step_body, pos0=pos0),
        grid=(1,),
        in_specs=[
            pl.BlockSpec((1, n_hist, nb, POOL_W), lambda i: (layer, 0, 0, 0)),
            pl.BlockSpec((t_new, nb, POOL_W), lambda i: (0, 0, 0)),
            pl.BlockSpec((None, len(POOL_WINDOWS), POOL_GROUP_W, POOL_GROUP_W), lambda i: (layer, 0, 0, 0)),
            pl.BlockSpec((None, 1, POOL_W), lambda i: (layer, 0, 0)),
        ],
        out_specs=pl.BlockSpec((t_new, nb, POOL_W), lambda i: (0, 0, 0)),
        out_shape=jax.ShapeDtypeStruct((t_new, nb, POOL_W), F32),
        compiler_params=_cparams("arbitrary"),
        name="pool_sample",
    )(hist, u_new, w_grp, scale)


def _attn_body(qt_ref, kt_ref, vt_ref, ct_ref, o_ref, qa_ref, ka_ref, va_ref, kpad_ref, *, blk, q_lanes):
    hp = pl.program_id(1)
    seq_len = qt_ref.shape[2]
    nblk = ka_ref.shape[1]
    q_pad = qa_ref.shape[2]

    row = lax.broadcasted_iota(jnp.int32, (LANES, seq_len), 0)
    qpair = qt_ref[0].astype(F32)
    kpair = kt_ref[0]
    vpair = vt_ref[0].astype(BF16)
    n_full = seq_len // blk
    tail = seq_len - n_full * blk
    assert nblk == n_full + 1 and 0 < tail

    for j in range(n_full):
        va_ref[j] = vpair[:, j * blk:(j + 1) * blk]
    va_ref[n_full] = jnp.zeros((LANES, blk), BF16)
    va_ref[n_full, :, 0:tail] = vpair[:, n_full * blk:seq_len]

    kpad_ref[:, n_full * blk:nblk * blk] = jnp.zeros((LANES, blk), F32)
    qa_ref[...] = jnp.zeros(qa_ref.shape, BF16)
    for hh in range(2):
        f0 = HEAD_DIM * (1 - hh)
        c1, c2, c3 = _split3(ct_ref[0, pl.ds(2 * hp + hh, 1), :] * LOG2E)
        in_head = (row >= HEAD_DIM * hh) & (row < HEAD_DIM * (hh + 1))
        qa = jnp.where(in_head, qpair,
             jnp.where(row == f0, c1, jnp.where(row == f0 + 1, c2, jnp.where(row == f0 + 2, c3,
             jnp.where((row >= f0 + 3) & (row < f0 + 6), 1.0, 0.0)))))
        ka = jnp.where(in_head, kpair,
             jnp.where((row >= f0) & (row < f0 + 3), 1.0,
             jnp.where(row == f0 + 3, -c1, jnp.where(row == f0 + 4, -c2, jnp.where(row == f0 + 5, -c3, 0.0)))))
        qa_ref[hh, :, 0:seq_len] = qa.astype(BF16)
        kpad_ref[:, 0:seq_len] = ka
        for j in range(nblk):
            ka_ref[hh, j] = jnp.transpose(kpad_ref[:, j * blk:(j + 1) * blk]).astype(BF16)

    tail_w = pl.cdiv(tail, LANES) * LANES

    def update(state, s, j, hh):
        m, l, acc = state
        m_new = jnp.maximum(m, jnp.max(s, axis=0, keepdims=True))
        alpha = jnp.exp2(m - m_new)
        p = jnp.exp2(s - m_new)
        l = alpha * l + jnp.sum(p, axis=0, keepdims=True)
        acc = alpha * acc + _dot(va_ref[j, HEAD_DIM * hh:HEAD_DIM * (hh + 1), 0:s.shape[0]], p.astype(BF16))
        return m_new, l, acc

    n_q = max(q_pad // q_lanes, 1)
    for i in range(n_q):
        q0 = i * q_lanes
        q1 = q_pad if i == n_q - 1 else q0 + q_lanes
        r1 = min(q1, seq_len)
        n = q1 - q0
        qb = [qa_ref[hh, :, q0:q1] for hh in range(2)]
        n_open = q0 // blk
        n_need = pl.cdiv(r1, blk)

        def open_block(j, states, qb=qb):
            return tuple(update(states[hh], _dot(ka_ref[hh, j], qb[hh]), j, hh) for hh in range(2))

        init = (jnp.full((1, n), NEG, F32), jnp.zeros((1, n), F32), jnp.zeros((HEAD_DIM, n), F32))
        states = lax.fori_loop(0, n_open, open_block, (init, init), unroll=True)

        for j in range(n_open, n_need):
            kb = blk if j < n_full else tail_w
            key = j * blk + lax.broadcasted_iota(jnp.int32, (kb, n), 0)
            query = q0 + lax.broadcasted_iota(jnp.int32, (kb, n), 1)
            states = tuple(update(states[hh], jnp.where(key <= query, _dot(ka_ref[hh, j, 0:kb, :], qb[hh]), NEG),
                                  j, hh) for hh in range(2))
        (_, l0, acc0), (_, l1, acc1) = states
        out = jnp.transpose(jnp.concatenate([acc0 / l0, acc1 / l1], axis=0))
        o_ref[0, q0:r1, :] = out[0:r1 - q0, :].astype(o_ref.dtype)


def _attn_prompt(qt, k_all, v_all, ct, layer):
    b, _, seq_len = qt.shape
    blk = ATT_BLOCK
    nblk = pl.cdiv(seq_len, blk)
    q_pad = pl.cdiv(seq_len, LANES) * LANES
    pair = lambda i, p: (i, p, 0)
    pair_l = lambda i, p: (layer, i, p, 0)
    return pl.pallas_call(
        functools.partial(_attn_body, blk=blk, q_lanes=ATT_Q_ROWS),
        grid=(b, N_HEADS // 2),
        in_specs=[
            pl.BlockSpec((1, LANES, seq_len), pair),
            pl.BlockSpec((None, 1, LANES, seq_len), pair_l),
            pl.BlockSpec((None, 1, LANES, seq_len), pair_l),
            pl.BlockSpec((1, N_HEADS, seq_len), lambda i, p: (i, 0, 0)),
        ],
        out_specs=pl.BlockSpec((1, seq_len, LANES), lambda i, p: (i, 0, p)),
        out_shape=jax.ShapeDtypeStruct((b, seq_len, ATT_W), BF16),
        scratch_shapes=[
            pltpu.VMEM((2, LANES, q_pad), BF16),
            pltpu.VMEM((2, nblk, blk, LANES), BF16),
            pltpu.VMEM((nblk, LANES, blk), BF16),
            pltpu.VMEM((LANES, nblk * blk), F32),
        ],
        compiler_params=_cparams("arbitrary", "arbitrary"),
        name="attn_prompt",
    )(qt, k_all, v_all, ct)


def _decode_body(pt_ref, q_ref, kn_ref, vn_ref, lfn_ref, *rest, n_pg, t_new):
    k_refs = rest[0:n_pg]
    v_refs = rest[n_pg:2 * n_pg]
    f_refs = rest[2 * n_pg:3 * n_pg]
    o_ref = rest[3 * n_pg]
    qbd_ref, m_ref, l_ref, acc_ref, carry_ref, cnew_ref = rest[3 * n_pg + 1:]
    j = pl.program_id(1)
    n_rows = t_new * N_HEADS

    row_h = lax.broadcasted_iota(jnp.int32, (n_rows, ATT_W), 0) & (N_HEADS - 1)
    own = (lax.broadcasted_iota(jnp.int32, (n_rows, ATT_W), 1) >> HEAD_SHIFT) == row_h

    r_i = lax.broadcasted_iota(jnp.int32, (n_rows, n_rows), 0)
    c_i = lax.broadcasted_iota(jnp.int32, (n_rows, n_rows), 1)
    same_head = (r_i & (N_HEADS - 1)) == (c_i & (N_HEADS - 1))

    def new_token_sum(lo):
        sel = same_head & ((c_i >> TOK_SHIFT) > lo) & ((c_i >> TOK_SHIFT) <= (r_i >> TOK_SHIFT))
        return jnp.sum(jnp.where(sel, lfn_ref[0], 0.0), axis=1, keepdims=True)

    @pl.when(j == 0)
    def _():
        q4 = q_ref[0]
        rep = jnp.concatenate([jnp.broadcast_to(q4[t:t + 1], (N_HEADS, ATT_W)) for t in range(t_new)], axis=0)
        qbd_ref[...] = jnp.where(own, rep, 0.0).astype(BF16)
        m_ref[...] = jnp.full_like(m_ref, NEG)
        l_ref[...] = jnp.zeros_like(l_ref)
        acc_ref[...] = jnp.zeros_like(acc_ref)
        carry_ref[...] = jnp.zeros_like(carry_ref)
        cnew_ref[...] = new_token_sum(-1)

    def update(s, pv):
        m = m_ref[...]
        m_new = jnp.maximum(m, jnp.max(s, axis=1, keepdims=True))
        alpha = jnp.exp(m - m_new)
        p = jnp.exp(s - m_new)
        l_ref[...] = alpha * l_ref[...] + jnp.sum(p, axis=1, keepdims=True)
        p = p.astype(BF16)
        out = pv(p[:, 0:PAGE], 0)
        for k in range(1, s.shape[1] // PAGE):
            out = out + pv(p[:, k * PAGE:(k + 1) * PAGE], k)
        acc_ref[...] = alpha * acc_ref[...] + out
        m_ref[...] = m_new

    later = (lax.broadcasted_iota(jnp.int32, (PAGE, PAGE), 0) >
             lax.broadcasted_iota(jnp.int32, (PAGE, PAGE), 1)).astype(BF16)
    qbd = qbd_ref[...]
    f_all = jnp.concatenate([f_refs[p][...] for p in range(n_pg)], axis=0)
    a, b, c = _split3(f_all)
    cs = _dot(jnp.concatenate([a, b, c], axis=0).astype(BF16), later)
    nf = n_pg * N_HEADS
    within = cs[0:nf] + cs[nf:2 * nf] + cs[2 * nf:3 * nf]
    total = jnp.sum(f_all, axis=1, keepdims=True)
    carry = carry_ref[...]
    tiles = []
    for p in range(n_pg):
        rows = slice(p * N_HEADS, (p + 1) * N_HEADS)
        suffix = within[rows] + carry
        carry = carry + total[rows]
        kt = k_refs[p][...].reshape(ATT_W, PAGE).astype(BF16)
        tiles.append(_dot(qbd, kt) + jnp.concatenate([suffix] * t_new, axis=0))
    carry_ref[...] = carry
    s = jnp.concatenate(tiles, axis=1) + cnew_ref[...]
    update(s, lambda pb, p: _dot_nt(pb, v_refs[p][...].reshape(ATT_W, PAGE).astype(BF16)))

    @pl.when(j == pl.num_programs(1) - 1)
    def _():
        zpad = jnp.zeros((PAGE - t_new, ATT_W), F32)
        kpad = jnp.concatenate([kn_ref[0], zpad], axis=0).astype(BF16)
        vpad = jnp.concatenate([vn_ref[0], zpad], axis=0).astype(BF16)
        lane = lax.broadcasted_iota(jnp.int32, (n_rows, PAGE), 1)
        tok = lax.broadcasted_iota(jnp.int32, (n_rows, PAGE), 0) >> TOK_SHIFT
        bias = jnp.zeros((n_rows, PAGE), F32)
        for t in range(t_new):
            bias = jnp.where(lane == t, new_token_sum(t), bias)
        s = jnp.where(lane <= tok, _dot_nt(qbd, kpad) + bias, NEG)
        update(s, lambda pb, k: _dot(pb, vpad))
        out = jnp.where(own, acc_ref[...] / l_ref[...], 0.0)
        o_ref[0] = jnp.sum(out.reshape(t_new, N_HEADS, ATT_W), axis=1)


def _decode(page_table, q, k_new, v_new, lf_new, ckt, cvt, cft, layer):
    nb, t_new, _ = q.shape
    n_pages = page_table.shape[1]
    n_pg = DEC_PAGES_PER_STEP
    n_rows = t_new * N_HEADS

    def page_spec(p, shape):
        zeros = (0,) * len(shape)

        def index(b, j, pt):
            return (layer, pt[b, n_pages - 1 - (j * n_pg + p)]) + zeros

        return pl.BlockSpec((None, None) + shape, index)

    new = lambda b, j, pt: (b, 0, 0)
    in_specs = [pl.BlockSpec((1, t_new, ATT_W), new), pl.BlockSpec((1, t_new, ATT_W), new),
                pl.BlockSpec((1, t_new, ATT_W), new), pl.BlockSpec((1, 1, n_rows), new)]
    in_specs += [page_spec(p, (N_HEADS, HEAD_DIM, PAGE)) for p in range(n_pg)]
    in_specs += [page_spec(p, (N_HEADS, HEAD_DIM, PAGE)) for p in range(n_pg)]
    in_specs += [page_spec(p, (N_HEADS, PAGE)) for p in range(n_pg)]
    grid_spec = pltpu.PrefetchScalarGridSpec(
        num_scalar_prefetch=1,
        grid=(nb, n_pages // n_pg),
        in_specs=in_specs,
        out_specs=pl.BlockSpec((1, t_new, ATT_W), new),
        scratch_shapes=[
            pltpu.VMEM((n_rows, ATT_W), BF16),
            pltpu.VMEM((n_rows, 1), F32),
            pltpu.VMEM((n_rows, 1), F32),
            pltpu.VMEM((n_rows, ATT_W), F32),
            pltpu.VMEM((N_HEADS, 1), F32),
            pltpu.VMEM((n_rows, 1), F32),
        ],
    )
    return pl.pallas_call(
        functools.partial(_decode_body, n_pg=n_pg, t_new=t_new),
        grid_spec=grid_spec,
        out_shape=jax.ShapeDtypeStruct((nb, t_new, ATT_W), F32),
        compiler_params=_cparams("arbitrary", "arbitrary"),
        name="attn_decode",
    )(page_table, q, k_new, v_new, lf_new, *([ckt] * n_pg), *([cvt] * n_pg), *([cft] * n_pg))


def _merge_body(x_ref, pu_ref, ao_ref, gpre_ref, wg_ref, wpu_ref, wau_ref, wout_ref, gpost_ref, o_ref, m_ref,
                *, chunk):
    x = x_ref[...]
    d = x.shape[1]
    h = _rms(x, gpre_ref[...]).astype(BF16)
    pu = pu_ref[...].astype(BF16)
    ao = ao_ref[...].astype(BF16)
    for c in range(d // chunk):
        sl = slice(c * chunk, (c + 1) * chunk)
        sl_att = slice(d + c * chunk, d + (c + 1) * chunk)
        m = (jax.nn.sigmoid(_dot(h, wg_ref[:, sl])) * _dot(pu, wpu_ref[:, sl])
             + jax.nn.sigmoid(_dot(h, wg_ref[:, sl_att])) * _dot(ao, wau_ref[:, sl]))
        m_ref[:, sl] = m.astype(BF16)
    y = _dot(m_ref[...], wout_ref[...])
    o_ref[...] = x + _rms(y, gpost_ref[...])


def _merge(x, pu, ao, g_pre, wg, wpu, wau, wout, g_post, layer, tm):
    n, d = x.shape
    row = lambda i: (i, 0)
    const = lambda i: (layer, 0, 0)
    return pl.pallas_call(
        functools.partial(_merge_body, chunk=256),
        grid=(n // tm,),
        in_specs=[
            pl.BlockSpec((tm, d), row),
            pl.BlockSpec((tm, POOL_W), row),
            pl.BlockSpec((tm, ATT_W), row),
            pl.BlockSpec((None, 1, d), const),
            pl.BlockSpec((None, d, 2 * d), const),
            pl.BlockSpec((None, POOL_W, d), const),
            pl.BlockSpec((None, ATT_W, d), const),
            pl.BlockSpec((None, d, d), const),
            pl.BlockSpec((None, 1, d), const),
        ],
        out_specs=pl.BlockSpec((tm, d), row),
        out_shape=jax.ShapeDtypeStruct((n, d), F32),
        scratch_shapes=[pltpu.VMEM((tm, d), BF16)],
        compiler_params=_cparams("arbitrary"),
        name="merge",
    )(x, pu, ao, g_pre, wg, wpu, wau, wout, g_post)


def _row_tile(n, cap):
    best = None
    for t in range(16, cap + 1, 16):
        if n % t == 0:
            best = t
    assert best is not None, n
    return best


def kernel(x_prompt, x_sample, cache_k, cache_v, cache_logf, state_pool, page_table, meta_tokens,
           ln_ffn1_pre, ln_ffn1_post, ffn1_wi, ffn1_wo, ln_mix_pre, ln_mix_post, w_in, b_forget,
           pool_w_grp, pool_scale, w_pool_up, w_att_up, w_out, ln_ffn2_pre, ln_ffn2_post,
           ffn2_wi, ffn2_wo):
    nbp, seq, d = x_prompt.shape
    nbs, t_new, _ = x_sample.shape
    depth = w_in.shape[0]
    n_pages = page_table.shape[1]
    past_len = n_pages * PAGE
    seq_len = seq + N_META

    meta = jnp.broadcast_to(meta_tokens.astype(x_prompt.dtype)[None], (nbp, N_META, d))
    xp = jnp.concatenate([meta, x_prompt], axis=1).reshape(nbp * seq_len, d)
    xs = x_sample.reshape(nbs * t_new, d)
    tm_p = _row_tile(nbp * seq_len, 768)
    tm_s = nbs * t_new

    bf = lambda w: w.astype(BF16)
    vec = lambda g: g[:, None, :]
    o1, o2, o3, o4, o5 = POOL_W, POOL_W + ATT_W, POOL_W + 2 * ATT_W, POOL_W + 3 * ATT_W, POOL_W + 3 * ATT_W + N_HEADS
    w_in_b = bf(w_in)
    w_f_pad = jnp.pad(w_in_b[:, :, o4:o5], ((0, 0), (0, 0), (0, LANES - N_HEADS)))
    w_u_b = w_in_b[:, :, 0:o1]
    w_rows_s = jnp.concatenate([w_in_b[:, :, 0:o4], w_f_pad], axis=2)
    w_gate_b = w_in_b[:, :, o5:]
    w_qkv_t = jnp.transpose(w_in_b[:, :, o1:o4], (0, 2, 1))
    w_f_t = jnp.pad(jnp.transpose(w_in_b[:, :, o4:o5], (0, 2, 1)), ((0, 0), (0, 16 - N_HEADS), (0, 0)))
    bf_row = jnp.pad(b_forget, ((0, 0), (0, LANES - N_HEADS)))[:, None, :]
    bf_col = b_forget[:, :, None]
    ffn1_wi_b, ffn1_wo_b, ffn2_wi_b, ffn2_wo_b = bf(ffn1_wi), bf(ffn1_wo), bf(ffn2_wi), bf(ffn2_wo)
    w_grp_b, wpu_b, wau_b, wout_b = bf(pool_w_grp), bf(w_pool_up), bf(w_att_up), bf(w_out)
    scale3 = vec(pool_scale)
    g1pre, g1post, gmpre, gmpost, g2pre, g2post = (vec(g) for g in (
        ln_ffn1_pre, ln_ffn1_post, ln_mix_pre, ln_mix_post, ln_ffn2_pre, ln_ffn2_post))

    ckt = jnp.transpose(cache_k, (0, 1, 3, 4, 2))
    cvt = jnp.transpose(cache_v, (0, 1, 3, 4, 2))
    cft = jnp.transpose(cache_logf, (0, 1, 3, 2))
    hist = jnp.transpose(state_pool, (0, 2, 1, 3))

    k_all = jnp.zeros((depth, nbp, ATT_W, seq_len), F32)
    v_all = jnp.zeros((depth, nbp, ATT_W, seq_len), F32)
    fp_l, pp_l, ks_l, vs_l, fs_l, ps_l = [], [], [], [], [], []
    for l in range(depth):
        xp = _ffn(xp, g1pre, g1post, ffn1_wi_b, ffn1_wo_b, l, tm_p)
        xs = _ffn(xs, g1pre, g1post, ffn1_wi_b, ffn1_wo_b, l, tm_s)

        xp3 = xp.reshape(nbp, seq_len, d)
        qt, k_all, v_all, lft, ct = _inproj_t(xp3, gmpre, w_qkv_t, w_f_t, bf_col, k_all, v_all, l, KV_TILE)
        pu, u_tail = _pool_seq(xp3, gmpre, w_u_b, w_grp_b, scale3, l)
        ao = _attn_prompt(qt, k_all, v_all, ct, l)
        xp = _merge(xp, pu.reshape(-1, POOL_W), ao.reshape(-1, ATT_W), gmpre, w_gate_b, wpu_b, wau_b, wout_b,
                    gmpost, l, tm_p)
        fp_l.append(lft); pp_l.append(u_tail)

        u, q, k, v, lf = _inproj_rows(xs, gmpre, w_rows_s, bf_row, l, tm_s)
        u_tb = jnp.transpose(u.reshape(nbs, t_new, POOL_W), (1, 0, 2))
        pu = jnp.transpose(_pool_step(hist, u_tb, w_grp_b, scale3, l, past_len), (1, 0, 2))
        lf8 = lf[:, 0:N_HEADS].reshape(nbs, t_new, N_HEADS)
        ao = _decode(page_table, q.reshape(nbs, t_new, ATT_W), k.reshape(nbs, t_new, ATT_W),
                     v.reshape(nbs, t_new, ATT_W), lf8.reshape(nbs, 1, t_new * N_HEADS), ckt, cvt, cft, l)
        xs = _merge(xs, pu.reshape(-1, POOL_W), ao.reshape(-1, ATT_W), gmpre, w_gate_b, wpu_b, wau_b, wout_b,
                    gmpost, l, tm_s)
        ks_l.append(k.reshape(nbs, t_new, N_HEADS, HEAD_DIM)); vs_l.append(v.reshape(nbs, t_new, N_HEADS, HEAD_DIM))
        fs_l.append(lf8)
        ps_l.append(jnp.concatenate([state_pool[l], u.reshape(nbs, t_new, POOL_W)], axis=1)[:, t_new:])

        xp = _ffn(xp, g2pre, g2post, ffn2_wi_b, ffn2_wo_b, l, tm_p)
        xs = _ffn(xs, g2pre, g2post, ffn2_wi_b, ffn2_wo_b, l, tm_s)

    y_prompt = xp.reshape(nbp, seq_len, d)[:, N_META:]
    y_sample = xs.reshape(nbs, t_new, d)
    heads_t = lambda a: jnp.transpose(a.reshape(depth, nbp, N_HEADS, HEAD_DIM, seq_len), (0, 1, 4, 2, 3))
    return (y_prompt, y_sample,
            heads_t(k_all), heads_t(v_all), jnp.transpose(jnp.stack(fp_l), (0, 1, 3, 2)), jnp.stack(pp_l),
            jnp.stack(ks_l), jnp.stack(vs_l), jnp.stack(fs_l), jnp.stack(ps_l))
```

```python
import functools

import jax
import jax.numpy as jnp
from jax import lax
from jax.experimental import pallas as pl
from jax.experimental.pallas import tpu as pltpu

F32 = jnp.float32
BF16 = jnp.bfloat16

RMS_EPS = 1e-6
N_META = 16
N_HEADS = 8
HEAD_DIM = 64
HEAD_SHIFT = 6
TOK_SHIFT = 3
ATT_W = N_HEADS * HEAD_DIM
POOL_WINDOWS = (2, 4, 8, 16)
POOL_GROUP_W = 128
POOL_W = len(POOL_WINDOWS) * POOL_GROUP_W
POOL_HIST = max(POOL_WINDOWS) - 1
HIST_ROWS = 16
PAGE = 128
LANES = 128
NEG = float(jnp.finfo(jnp.float32).min)
LOG2E = 1.4426950408889634

ATT_BLOCK = 1024
ATT_Q_ROWS = 1024
KV_TILE = 512
DEC_PAGES_PER_STEP = 32
VMEM_LIMIT = 56 * 1024 * 1024


def _cparams(*sem):
    return pltpu.CompilerParams(dimension_semantics=sem, vmem_limit_bytes=VMEM_LIMIT)


def _rms(x, g):
    ms = jnp.mean(x * x, axis=-1, keepdims=True)
    return x * lax.rsqrt(ms + RMS_EPS) * g


def _log_sigmoid(x):
    return -(jnp.maximum(-x, 0.0) + jnp.log1p(jnp.exp(-jnp.abs(x))))


def _split3(x):
    a = x.astype(BF16).astype(F32)
    r = x - a
    b = r.astype(BF16).astype(F32)
    c = (r - b).astype(BF16).astype(F32)
    return a, b, c


def _dot(a, b):
    return jnp.dot(a, b, preferred_element_type=F32)


def _dot_nt(a, b):
    return lax.dot_general(a, b, (((1,), (1,)), ((), ())), preferred_element_type=F32)


def _ffn_body(x_ref, gpre_ref, gpost_ref, wig_ref, wiu_ref, wo_ref, o_ref, a_ref, *, chunk):
    x = x_ref[...]
    h = _rms(x, gpre_ref[...]).astype(BF16)
    d_ff = a_ref.shape[1]
    for c in range(d_ff // chunk):
        sl = slice(c * chunk, (c + 1) * chunk)
        g = _dot(h, wig_ref[:, sl])
        u = _dot(h, wiu_ref[:, sl])
        a_ref[:, sl] = (g * jax.nn.sigmoid(g) * u).astype(BF16)
    y = _dot(a_ref[...], wo_ref[...])
    o_ref[...] = x + 0.5 * _rms(y, gpost_ref[...])


def _ffn(x, g_pre, g_post, wi, wo, layer, tm):
    n, d = x.shape
    d_ff = wo.shape[1]
    const = lambda i: (layer, 0, 0)
    return pl.pallas_call(
        functools.partial(_ffn_body, chunk=256),
        grid=(n // tm,),
        in_specs=[
            pl.BlockSpec((tm, d), lambda i: (i, 0)),
            pl.BlockSpec((None, 1, d), const),
            pl.BlockSpec((None, 1, d), const),
            pl.BlockSpec((None, d, d_ff), const),
            pl.BlockSpec((None, d, d_ff), lambda i: (layer, 0, 1)),
            pl.BlockSpec((None, d_ff, d), const),
        ],
        out_specs=pl.BlockSpec((tm, d), lambda i: (i, 0)),
        out_shape=jax.ShapeDtypeStruct((n, d), F32),
        scratch_shapes=[pltpu.VMEM((tm, d_ff), BF16)],
        compiler_params=_cparams("arbitrary"),
        name="ffn_half_step",
    )(x, g_pre, g_post, wi, wi, wo)


def _inproj_rows_body(x_ref, g_ref, w_ref, bf_ref, u_ref, q_ref, k_ref, v_ref, lf_ref):
    h = _rms(x_ref[...], g_ref[...]).astype(BF16)
    u_ref[...] = _dot(h, w_ref[:, 0:POOL_W])
    q_ref[...] = _dot(h, w_ref[:, POOL_W:POOL_W + ATT_W]) * (HEAD_DIM ** -0.5)
    k0 = POOL_W + ATT_W
    k_ref[...] = _dot(h, w_ref[:, k0:k0 + ATT_W])
    v_ref[...] = _dot(h, w_ref[:, k0 + ATT_W:k0 + 2 * ATT_W])
    zf = _dot(h, w_ref[:, k0 + 2 * ATT_W:k0 + 2 * ATT_W + LANES])
    lf_ref[...] = _log_sigmoid(zf + bf_ref[...])


def _inproj_rows(x, g, w, bf, layer, tm):
    n, d = x.shape
    wcols = w.shape[2]
    const = lambda i: (layer, 0, 0)
    row = lambda i: (i, 0)
    widths = (POOL_W, ATT_W, ATT_W, ATT_W, LANES)
    return pl.pallas_call(
        _inproj_rows_body,
        grid=(n // tm,),
        in_specs=[
            pl.BlockSpec((tm, d), row),
            pl.BlockSpec((None, 1, d), const),
            pl.BlockSpec((None, d, wcols), const),
            pl.BlockSpec((None, 1, LANES), const),
        ],
        out_specs=[pl.BlockSpec((tm, w_), row) for w_ in widths],
        out_shape=[jax.ShapeDtypeStruct((n, w_), F32) for w_ in widths],
        compiler_params=_cparams("arbitrary"),
        name="inproj_rows",
    )(x, g, w, bf)


def _inproj_t_body(x_ref, g_ref, wt_ref, wft_ref, bft_ref, tri_ref, kall_ref, vall_ref,
                   qt_ref, kt_ref, vt_ref, lft_ref, ct_ref, *, tl):
    del kall_ref, vall_ref
    seq_len = x_ref.shape[1]
    n_full = seq_len // tl
    rem = seq_len - n_full * tl
    win = pl.cdiv(rem, LANES) * LANES
    assert win <= tl
    carry = jnp.zeros((N_HEADS, 1), F32)
    chunks = [(c * tl, tl, 0) for c in range(n_full)] + ([(seq_len - win, win, win - rem)] if rem else [])
    for p0, width, skip in chunks:
        h = _rms(x_ref[0, p0:p0 + width, :], g_ref[...]).astype(BF16)
        zt = _dot_nt(wt_ref[...], h)
        lf = _log_sigmoid(_dot_nt(wft_ref[...], h)[0:N_HEADS] + bft_ref[...])
        fresh = lax.broadcasted_iota(jnp.int32, lf.shape, 1) >= skip
        a, b, c = _split3(jnp.where(fresh, lf, 0.0))
        stack = jnp.concatenate([a, b, c, jnp.zeros_like(a)], axis=0).astype(BF16)
        cs = _dot(stack, tri_ref[0:width, 0:width])
        run = cs[0:8] + cs[8:16] + cs[16:24] + carry
        carry = run[:, width - 1:width]
        dst = slice(p0 + skip, p0 + width)
        qt_ref[0, :, dst] = (zt[0:ATT_W, skip:] * (LOG2E * HEAD_DIM ** -0.5)).astype(qt_ref.dtype)
        kt_ref[0, :, dst] = zt[ATT_W:2 * ATT_W, skip:]
        vt_ref[0, :, dst] = zt[2 * ATT_W:3 * ATT_W, skip:]
        lft_ref[0, :, dst] = lf[:, skip:]
        ct_ref[0, :, dst] = run[:, skip:]


def _inproj_t(xp, g, wt, wft, bft, k_all, v_all, layer, tl):
    b, seq_len, d = xp.shape
    tri = (lax.broadcasted_iota(jnp.int32, (tl, tl), 0) <= lax.broadcasted_iota(jnp.int32, (tl, tl), 1)).astype(BF16)
    const = lambda i: (layer, 0, 0)
    seq = lambda i: (i, 0, 0)
    slab = lambda i: (layer, i, 0, 0)
    return pl.pallas_call(
        functools.partial(_inproj_t_body, tl=tl),
        grid=(b,),
        in_specs=[
            pl.BlockSpec((1, seq_len, d), seq),
            pl.BlockSpec((None, 1, d), const),
            pl.BlockSpec((None, 3 * ATT_W, d), const),
            pl.BlockSpec((None, 16, d), const),
            pl.BlockSpec((None, N_HEADS, 1), const),
            pl.BlockSpec((tl, tl), lambda i: (0, 0)),
            pl.BlockSpec(memory_space=pl.ANY),
            pl.BlockSpec(memory_space=pl.ANY),
        ],
        out_specs=[
            pl.BlockSpec((1, ATT_W, seq_len), seq),
            pl.BlockSpec((None, 1, ATT_W, seq_len), slab),
            pl.BlockSpec((None, 1, ATT_W, seq_len), slab),
            pl.BlockSpec((1, N_HEADS, seq_len), seq),
            pl.BlockSpec((1, N_HEADS, seq_len), seq),
        ],
        out_shape=[
            jax.ShapeDtypeStruct((b, ATT_W, seq_len), BF16),
            jax.ShapeDtypeStruct(k_all.shape, F32),
            jax.ShapeDtypeStruct(v_all.shape, F32),
            jax.ShapeDtypeStruct((b, N_HEADS, seq_len), F32),
            jax.ShapeDtypeStruct((b, N_HEADS, seq_len), F32),
        ],
        input_output_aliases={6: 1, 7: 2},
        compiler_params=_cparams("arbitrary"),
        name="inproj_feature_major",
    )(xp, g, wt, wft, bft, tri, k_all, v_all)


def _pool_mix_group(tot, cur, cnt, w_ref, scale_ref, g):
    cs = slice(g * POOL_GROUP_W, (g + 1) * POOL_GROUP_W)
    d = tot / cnt - cur
    return _dot(d.astype(BF16), w_ref[g]) * scale_ref[:, cs]


def _pool_seq_body(x_ref, g_ref, wu_ref, w_ref, scale_ref, o_ref, tail_ref, pad_ref, *, chunk):
    seq_len = x_ref.shape[1]
    pad_ref[0:HIST_ROWS, :] = jnp.zeros((HIST_ROWS, POOL_W), F32)
    for r0 in range(0, seq_len, chunk):
        h = _rms(x_ref[0, r0:r0 + chunk, :], g_ref[...]).astype(BF16)
        pad_ref[HIST_ROWS + r0:HIST_ROWS + r0 + chunk, :] = _dot(h, wu_ref[...])
    tail_ref[0] = pad_ref[HIST_ROWS + seq_len - POOL_HIST:HIST_ROWS + seq_len, :]
    for r0 in range(0, seq_len, chunk):
        pos = r0 + lax.broadcasted_iota(jnp.int32, (chunk, 1), 0)
        for g, w in enumerate(POOL_WINDOWS):
            cs = slice(g * POOL_GROUP_W, (g + 1) * POOL_GROUP_W)
            base = HIST_ROWS + r0
            cur = pad_ref[base:base + chunk, cs]
            tot = pad_ref[base - HIST_ROWS:base + chunk, cs]
            sh = 1
            while sh < w:
                tot = tot[sh:, :] + tot[:-sh, :]
                sh *= 2
            tot = tot[HIST_ROWS - (w - 1):, :]
            cnt = jnp.minimum(w, pos + 1).astype(F32)
            o_ref[0, r0:r0 + chunk, cs] = _pool_mix_group(tot, cur, cnt, w_ref, scale_ref, g).astype(o_ref.dtype)


def _pool_seq(x, g, wu, w_grp, scale, layer):
    b, seq_len, d = x.shape
    chunk = seq_len // 3
    const3 = lambda i: (layer, 0, 0)
    return pl.pallas_call(
        functools.partial(_pool_seq_body, chunk=chunk),
        grid=(b,),
        in_specs=[
            pl.BlockSpec((1, seq_len, d), lambda i: (i, 0, 0)),
            pl.BlockSpec((None, 1, d), const3),
            pl.BlockSpec((None, d, POOL_W), const3),
            pl.BlockSpec((None, len(POOL_WINDOWS), POOL_GROUP_W, POOL_GROUP_W), lambda i: (layer, 0, 0, 0)),
            pl.BlockSpec((None, 1, POOL_W), const3),
        ],
        out_specs=[pl.BlockSpec((1, seq_len, POOL_W), lambda i: (i, 0, 0)),
                   pl.BlockSpec((1, POOL_HIST, POOL_W), lambda i: (i, 0, 0))],
        out_shape=[jax.ShapeDtypeStruct((b, seq_len, POOL_W), BF16),
                   jax.ShapeDtypeStruct((b, POOL_HIST, POOL_W), F32)],
        scratch_shapes=[pltpu.VMEM((HIST_ROWS + seq_len, POOL_W), F32)],
        compiler_params=_cparams("arbitrary"),
        name="pool_prompt",
    )(x, g, wu, w_grp, scale)


def _pool_step_body(hist_ref, u_ref, w_ref, scale_ref, o_ref, *, pos0):
    n_hist = hist_ref.shape[1]
    t_new = u_ref.shape[0]

    def slab(i):
        return hist_ref[0, i] if i < n_hist else u_ref[i - n_hist]

    for t in range(t_new):
        end = n_hist + t
        for g, w in enumerate(POOL_WINDOWS):
            cs = slice(g * POOL_GROUP_W, (g + 1) * POOL_GROUP_W)
            cur = slab(end)[:, cs]
            tot = cur
            for i in range(1, w):
                if end - i >= 0:
                    tot = tot + slab(end - i)[:, cs]
            cnt = float(min(w, pos0 + t + 1))
            o_ref[t, :, cs] = _pool_mix_group(tot, cur, cnt, w_ref, scale_ref, g).astype(o_ref.dtype)


def _pool_step(hist, u_new, w_grp, scale, layer, pos0):
    _, n_hist, nb, _ = hist.shape
    t_new = u_new.shape[0]
    return pl.pallas_call(
        functools.partial(_pool_step_body, pos0=pos0),
        grid=(1,),
        in_specs=[
            pl.BlockSpec((1, n_hist, nb, POOL_W), lambda i: (layer, 0, 0, 0)),
            pl.BlockSpec((t_new, nb, POOL_W), lambda i: (0, 0, 0)),
            pl.BlockSpec((None, len(POOL_WINDOWS), POOL_GROUP_W, POOL_GROUP_W), lambda i: (layer, 0, 0, 0)),
            pl.BlockSpec((None, 1, POOL_W), lambda i: (layer, 0, 0)),
        ],
        out_specs=pl.BlockSpec((t_new, nb, POOL_W), lambda i: (0, 0, 0)),
        out_shape=jax.ShapeDtypeStruct((t_new, nb, POOL_W), F32),
        compiler_params=_cparams("arbitrary"),
        name="pool_sample",
    )(hist, u_new, w_grp, scale)


def _attn_body(qt_ref, kt_ref, vt_ref, ct_ref, o_ref, qa_ref, ka_ref, va_ref, kpad_ref, *, blk, q_lanes):
    hp = pl.program_id(1)
    seq_len = qt_ref.shape[2]
    nblk = ka_ref.shape[1]
    q_pad = qa_ref.shape[2]

    row = lax.broadcasted_iota(jnp.int32, (LANES, seq_len), 0)
    qpair = qt_ref[0].astype(F32)
    kpair = kt_ref[0]
    vpair = vt_ref[0].astype(BF16)
    n_full = seq_len // blk
    tail = seq_len - n_full * blk
    assert nblk == n_full + 1 and 0 < tail

    for j in range(n_full):
        va_ref[j] = vpair[:, j * blk:(j + 1) * blk]
    va_ref[n_full] = jnp.zeros((LANES, blk), BF16)
    va_ref[n_full, :, 0:tail] = vpair[:, n_full * blk:seq_len]

    kpad_ref[:, n_full * blk:nblk * blk] = jnp.zeros((LANES, blk), F32)
    qa_ref[...] = jnp.zeros(qa_ref.shape, BF16)
    for hh in range(2):
        f0 = HEAD_DIM * (1 - hh)
        c1, c2, c3 = _split3(ct_ref[0, pl.ds(2 * hp + hh, 1), :] * LOG2E)
        in_head = (row >= HEAD_DIM * hh) & (row < HEAD_DIM * (hh + 1))
        qa = jnp.where(in_head, qpair,
             jnp.where(row == f0, c1, jnp.where(row == f0 + 1, c2, jnp.where(row == f0 + 2, c3,
             jnp.where((row >= f0 + 3) & (row < f0 + 6), 1.0, 0.0)))))
        ka = jnp.where(in_head, kpair,
             jnp.where((row >= f0) & (row < f0 + 3), 1.0,
             jnp.where(row == f0 + 3, -c1, jnp.where(row == f0 + 4, -c2, jnp.where(row == f0 + 5, -c3, 0.0)))))
        qa_ref[hh, :, 0:seq_len] = qa.astype(BF16)
        kpad_ref[:, 0:seq_len] = ka
        for j in range(nblk):
            ka_ref[hh, j] = jnp.transpose(kpad_ref[:, j * blk:(j + 1) * blk]).astype(BF16)

    tail_w = pl.cdiv(tail, LANES) * LANES

    def update(state, s, j, hh):
        m, l, acc = state
        m_new = jnp.maximum(m, jnp.max(s, axis=0, keepdims=True))
        alpha = jnp.exp2(m - m_new)
        p = jnp.exp2(s - m_new)
        l = alpha * l + jnp.sum(p, axis=0, keepdims=True)
        acc = alpha * acc + _dot(va_ref[j, HEAD_DIM * hh:HEAD_DIM * (hh + 1), 0:s.shape[0]], p.astype(BF16))
        return m_new, l, acc

    n_q = max(q_pad // q_lanes, 1)
    for i in range(n_q):
        q0 = i * q_lanes
        q1 = q_pad if i == n_q - 1 else q0 + q_lanes
        r1 = min(q1, seq_len)
        n = q1 - q0
        qb = [qa_ref[hh, :, q0:q1] for hh in range(2)]
        n_open = q0 // blk
        n_need = pl.cdiv(r1, blk)

        def open_block(j, states, qb=qb):
            return tuple(update(states[hh], _dot(ka_ref[hh, j], qb[hh]), j, hh) for hh in range(2))

        init = (jnp.full((1, n), NEG, F32), jnp.zeros((1, n), F32), jnp.zeros((HEAD_DIM, n), F32))
        states = lax.fori_loop(0, n_open, open_block, (init, init), unroll=True)

        for j in range(n_open, n_need):
            kb = blk if j < n_full else tail_w
            key = j * blk + lax.broadcasted_iota(jnp.int32, (kb, n), 0)
            query = q0 + lax.broadcasted_iota(jnp.int32, (kb, n), 1)
            states = tuple(update(states[hh], jnp.where(key <= query, _dot(ka_ref[hh, j, 0:kb, :], qb[hh]), NEG),
                                  j, hh) for hh in range(2))
        (_, l0, acc0), (_, l1, acc1) = states
        out = jnp.transpose(jnp.concatenate([acc0 / l0, acc1 / l1], axis=0))
        o_ref[0, q0:r1, :] = out[0:r1 - q0, :].astype(o_ref.dtype)


def _attn_prompt(qt, k_all, v_all, ct, layer):
    b, _, seq_len = qt.shape
    blk = ATT_BLOCK
    nblk = pl.cdiv(seq_len, blk)
    q_pad = pl.cdiv(seq_len, LANES) * LANES
    pair = lambda i, p: (i, p, 0)
    pair_l = lambda i, p: (layer, i, p, 0)
    return pl.pallas_call(
        functools.partial(_attn_body, blk=blk, q_lanes=ATT_Q_ROWS),
        grid=(b, N_HEADS // 2),
        in_specs=[
            pl.BlockSpec((1, LANES, seq_len), pair),
            pl.BlockSpec((None, 1, LANES, seq_len), pair_l),
            pl.BlockSpec((None, 1, LANES, seq_len), pair_l),
            pl.BlockSpec((1, N_HEADS, seq_len), lambda i, p: (i, 0, 0)),
        ],
        out_specs=pl.BlockSpec((1, seq_len, LANES), lambda i, p: (i, 0, p)),
        out_shape=jax.ShapeDtypeStruct((b, seq_len, ATT_W), BF16),
        scratch_shapes=[
            pltpu.VMEM((2, LANES, q_pad), BF16),
            pltpu.VMEM((2, nblk, blk, LANES), BF16),
            pltpu.VMEM((nblk, LANES, blk), BF16),
            pltpu.VMEM((LANES, nblk * blk), F32),
        ],
        compiler_params=_cparams("arbitrary", "arbitrary"),
        name="attn_prompt",
    )(qt, k_all, v_all, ct)


def _decode_body(pt_ref, q_ref, kn_ref, vn_ref, lfn_ref, *rest, n_pg, t_new):
    k_refs = rest[0:n_pg]
    v_refs = rest[n_pg:2 * n_pg]
    f_refs = rest[2 * n_pg:3 * n_pg]
    o_ref = rest[3 * n_pg]
    qbd_ref, m_ref, l_ref, acc_ref, carry_ref, cnew_ref = rest[3 * n_pg + 1:]
    j = pl.program_id(1)
    n_rows = t_new * N_HEADS

    row_h = lax.broadcasted_iota(jnp.int32, (n_rows, ATT_W), 0) & (N_HEADS - 1)
    own = (lax.broadcasted_iota(jnp.int32, (n_rows, ATT_W), 1) >> HEAD_SHIFT) == row_h

    r_i = lax.broadcasted_iota(jnp.int32, (n_rows, n_rows), 0)
    c_i = lax.broadcasted_iota(jnp.int32, (n_rows, n_rows), 1)
    same_head = (r_i & (N_HEADS - 1)) == (c_i & (N_HEADS - 1))

    def new_token_sum(lo):
        sel = same_head & ((c_i >> TOK_SHIFT) > lo) & ((c_i >> TOK_SHIFT) <= (r_i >> TOK_SHIFT))
        return jnp.sum(jnp.where(sel, lfn_ref[0], 0.0), axis=1, keepdims=True)

    @pl.when(j == 0)
    def _():
        q4 = q_ref[0]
        rep = jnp.concatenate([jnp.broadcast_to(q4[t:t + 1], (N_HEADS, ATT_W)) for t in range(t_new)], axis=0)
        qbd_ref[...] = jnp.where(own, rep, 0.0).astype(BF16)
        m_ref[...] = jnp.full_like(m_ref, NEG)
        l_ref[...] = jnp.zeros_like(l_ref)
        acc_ref[...] = jnp.zeros_like(acc_ref)
        carry_ref[...] = jnp.zeros_like(carry_ref)
        cnew_ref[...] = new_token_sum(-1)

    def update(s, pv):
        m = m_ref[...]
        m_new = jnp.maximum(m, jnp.max(s, axis=1, keepdims=True))
        alpha = jnp.exp(m - m_new)
        p = jnp.exp(s - m_new)
        l_ref[...] = alpha * l_ref[...] + jnp.sum(p, axis=1, keepdims=True)
        p = p.astype(BF16)
        out = pv(p[:, 0:PAGE], 0)
        for k in range(1, s.shape[1] // PAGE):
            out = out + pv(p[:, k * PAGE:(k + 1) * PAGE], k)
        acc_ref[...] = alpha * acc_ref[...] + out
        m_ref[...] = m_new

    later = (lax.broadcasted_iota(jnp.int32, (PAGE, PAGE), 0) >
             lax.broadcasted_iota(jnp.int32, (PAGE, PAGE), 1)).astype(BF16)
    qbd = qbd_ref[...]
    f_all = jnp.concatenate([f_refs[p][...] for p in range(n_pg)], axis=0)
    a, b, c = _split3(f_all)
    cs = _dot(jnp.concatenate([a, b, c], axis=0).astype(BF16), later)
    nf = n_pg * N_HEADS
    within = cs[0:nf] + cs[nf:2 * nf] + cs[2 * nf:3 * nf]
    total = jnp.sum(f_all, axis=1, keepdims=True)
    carry = carry_ref[...]
    tiles = []
    for p in range(n_pg):
        rows = slice(p * N_HEADS, (p + 1) * N_HEADS)
        suffix = within[rows] + carry
        carry = carry + total[rows]
        kt = k_refs[p][...].reshape(ATT_W, PAGE).astype(BF16)
        tiles.append(_dot(qbd, kt) + jnp.concatenate([suffix] * t_new, axis=0))
    carry_ref[...] = carry
    s = jnp.concatenate(tiles, axis=1) + cnew_ref[...]
    update(s, lambda pb, p: _dot_nt(pb, v_refs[p][...].reshape(ATT_W, PAGE).astype(BF16)))

    @pl.when(j == pl.num_programs(1) - 1)
    def _():
        zpad = jnp.zeros((PAGE - t_new, ATT_W), F32)
        kpad = jnp.concatenate([kn_ref[0], zpad], axis=0).astype(BF16)
        vpad = jnp.concatenate([vn_ref[0], zpad], axis=0).astype(BF16)
        lane = lax.broadcasted_iota(jnp.int32, (n_rows, PAGE), 1)
        tok = lax.broadcasted_iota(jnp.int32, (n_rows, PAGE), 0) >> TOK_SHIFT
        bias = jnp.zeros((n_rows, PAGE), F32)
        for t in range(t_new):
            bias = jnp.where(lane == t, new_token_sum(t), bias)
        s = jnp.where(lane <= tok, _dot_nt(qbd, kpad) + bias, NEG)
        update(s, lambda pb, k: _dot(pb, vpad))
        out = jnp.where(own, acc_ref[...] / l_ref[...], 0.0)
        o_ref[0] = jnp.sum(out.reshape(t_new, N_HEADS, ATT_W), axis=1)


def _decode(page_table, q, k_new, v_new, lf_new, ckt, cvt, cft, layer):
    nb, t_new, _ = q.shape
    n_pages = page_table.shape[1]
    n_pg = DEC_PAGES_PER_STEP
    n_rows = t_new * N_HEADS

    def page_spec(p, shape):
        zeros = (0,) * len(shape)

        def index(b, j, pt):
            return (layer, pt[b, n_pages - 1 - (j * n_pg + p)]) + zeros

        return pl.BlockSpec((None, None) + shape, index)

    new = lambda b, j, pt: (b, 0, 0)
    in_specs = [pl.BlockSpec((1, t_new, ATT_W), new), pl.BlockSpec((1, t_new, ATT_W), new),
                pl.BlockSpec((1, t_new, ATT_W), new), pl.BlockSpec((1, 1, n_rows), new)]
    in_specs += [page_spec(p, (N_HEADS, HEAD_DIM, PAGE)) for p in range(n_pg)]
    in_specs += [page_spec(p, (N_HEADS, HEAD_DIM, PAGE)) for p in range(n_pg)]
    in_specs += [page_spec(p, (N_HEADS, PAGE)) for p in range(n_pg)]
    grid_spec = pltpu.PrefetchScalarGridSpec(
        num_scalar_prefetch=1,
        grid=(nb, n_pages // n_pg),
        in_specs=in_specs,
        out_specs=pl.BlockSpec((1, t_new, ATT_W), new),
        scratch_shapes=[
            pltpu.VMEM((n_rows, ATT_W), BF16),
            pltpu.VMEM((n_rows, 1), F32),
            pltpu.VMEM((n_rows, 1), F32),
            pltpu.VMEM((n_rows, ATT_W), F32),
            pltpu.VMEM((N_HEADS, 1), F32),
            pltpu.VMEM((n_rows, 1), F32),
        ],
    )
    return pl.pallas_call(
        functools.partial(_decode_body, n_pg=n_pg, t_new=t_new),
        grid_spec=grid_spec,
        out_shape=jax.ShapeDtypeStruct((nb, t_new, ATT_W), F32),
        compiler_params=_cparams("arbitrary", "arbitrary"),
        name="attn_decode",
    )(page_table, q, k_new, v_new, lf_new, *([ckt] * n_pg), *([cvt] * n_pg), *([cft] * n_pg))


def _merge_body(x_ref, pu_ref, ao_ref, gpre_ref, wg_ref, wpu_ref, wau_ref, wout_ref, gpost_ref, o_ref, m_ref,
                *, chunk):
    x = x_ref[...]
    d = x.shape[1]
    h = _rms(x, gpre_ref[...]).astype(BF16)
    pu = pu_ref[...].astype(BF16)
    ao = ao_ref[...].astype(BF16)
    for c in range(d // chunk):
        sl = slice(c * chunk, (c + 1) * chunk)
        sl_att = slice(d + c * chunk, d + (c + 1) * chunk)
        m = (jax.nn.sigmoid(_dot(h, wg_ref[:, sl])) * _dot(pu, wpu_ref[:, sl])
             + jax.nn.sigmoid(_dot(h, wg_ref[:, sl_att])) * _dot(ao, wau_ref[:, sl]))
        m_ref[:, sl] = m.astype(BF16)
    y = _dot(m_ref[...], wout_ref[...])
    o_ref[...] = x + _rms(y, gpost_ref[...])


def _merge(x, pu, ao, g_pre, wg, wpu, wau, wout, g_post, layer, tm):
    n, d = x.shape
    row = lambda i: (i, 0)
    const = lambda i: (layer, 0, 0)
    return pl.pallas_call(
        functools.partial(_merge_body, chunk=256),
        grid=(n // tm,),
        in_specs=[
            pl.BlockSpec((tm, d), row),
            pl.BlockSpec((tm, POOL_W), row),
            pl.BlockSpec((tm, ATT_W), row),
            pl.BlockSpec((None, 1, d), const),
            pl.BlockSpec((None, d, 2 * d), const),
            pl.BlockSpec((None, POOL_W, d), const),
            pl.BlockSpec((None, ATT_W, d), const),
            pl.BlockSpec((None, d, d), const),
            pl.BlockSpec((None, 1, d), const),
        ],
        out_specs=pl.BlockSpec((tm, d), row),
        out_shape=jax.ShapeDtypeStruct((n, d), F32),
        scratch_shapes=[pltpu.VMEM((tm, d), BF16)],
        compiler_params=_cparams("arbitrary"),
        name="merge",
    )(x, pu, ao, g_pre, wg, wpu, wau, wout, g_post)


def _row_tile(n, cap):
    best = None
    for t in range(16, cap + 1, 16):
        if n % t == 0:
            best = t
    assert best is not None, n
    return best


def kernel(x_prompt, x_sample, cache_k, cache_v, cache_logf, state_pool, page_table, meta_tokens,
           ln_ffn1_pre, ln_ffn1_post, ffn1_wi, ffn1_wo, ln_mix_pre, ln_mix_post, w_in, b_forget,
           pool_w_grp, pool_scale, w_pool_up, w_att_up, w_out, ln_ffn2_pre, ln_ffn2_post,
           ffn2_wi, ffn2_wo):
    nbp, seq, d = x_prompt.shape
    nbs, t_new, _ = x_sample.shape
    depth = w_in.shape[0]
    n_pages = page_table.shape[1]
    past_len = n_pages * PAGE
    seq_len = seq + N_META

    meta = jnp.broadcast_to(meta_tokens.astype(x_prompt.dtype)[None], (nbp, N_META, d))
    xp = jnp.concatenate([meta, x_prompt], axis=1).reshape(nbp * seq_len, d)
    xs = x_sample.reshape(nbs * t_new, d)
    tm_p = _row_tile(nbp * seq_len, 768)
    tm_s = nbs * t_new

    bf = lambda w: w.astype(BF16)
    vec = lambda g: g[:, None, :]
    o1, o2, o3, o4, o5 = POOL_W, POOL_W + ATT_W, POOL_W + 2 * ATT_W, POOL_W + 3 * ATT_W, POOL_W + 3 * ATT_W + N_HEADS
    w_in_b = bf(w_in)
    w_f_pad = jnp.pad(w_in_b[:, :, o4:o5], ((0, 0), (0, 0), (0, LANES - N_HEADS)))
    w_u_b = w_in_b[:, :, 0:o1]
    w_rows_s = jnp.concatenate([w_in_b[:, :, 0:o4], w_f_pad], axis=2)
    w_gate_b = w_in_b[:, :, o5:]
    w_qkv_t = jnp.transpose(w_in_b[:, :, o1:o4], (0, 2, 1))
    w_f_t = jnp.pad(jnp.transpose(w_in_b[:, :, o4:o5], (0, 2, 1)), ((0, 0), (0, 16 - N_HEADS), (0, 0)))
    bf_row = jnp.pad(b_forget, ((0, 0), (0, LANES - N_HEADS)))[:, None, :]
    bf_col = b_forget[:, :, None]
    ffn1_wi_b, ffn1_wo_b, ffn2_wi_b, ffn2_wo_b = bf(ffn1_wi), bf(ffn1_wo), bf(ffn2_wi), bf(ffn2_wo)
    w_grp_b, wpu_b, wau_b, wout_b = bf(pool_w_grp), bf(w_pool_up), bf(w_att_up), bf(w_out)
    scale3 = vec(pool_scale)
    g1pre, g1post, gmpre, gmpost, g2pre, g2post = (vec(g) for g in (
        ln_ffn1_pre, ln_ffn1_post, ln_mix_pre, ln_mix_post, ln_ffn2_pre, ln_ffn2_post))

    ckt = jnp.transpose(cache_k, (0, 1, 3, 4, 2))
    cvt = jnp.transpose(cache_v, (0, 1, 3, 4, 2))
    cft = jnp.transpose(cache_logf, (0, 1, 3, 2))
    hist = jnp.transpose(state_pool, (0, 2, 1, 3))

    k_all = jnp.zeros((depth, nbp, ATT_W, seq_len), F32)
    v_all = jnp.zeros((depth, nbp, ATT_W, seq_len), F32)
    fp_l, pp_l, ks_l, vs_l, fs_l, ps_l = [], [], [], [], [], []
    for l in range(depth):
        xp = _ffn(xp, g1pre, g1post, ffn1_wi_b, ffn1_wo_b, l, tm_p)
        xs = _ffn(xs, g1pre, g1post, ffn1_wi_b, ffn1_wo_b, l, tm_s)

        xp3 = xp.reshape(nbp, seq_len, d)
        qt, k_all, v_all, lft, ct = _inproj_t(xp3, gmpre, w_qkv_t, w_f_t, bf_col, k_all, v_all, l, KV_TILE)
        pu, u_tail = _pool_seq(xp3, gmpre, w_u_b, w_grp_b, scale3, l)
        ao = _attn_prompt(qt, k_all, v_all, ct, l)
        xp = _merge(xp, pu.reshape(-1, POOL_W), ao.reshape(-1, ATT_W), gmpre, w_gate_b, wpu_b, wau_b, wout_b,
                    gmpost, l, tm_p)
        fp_l.append(lft); pp_l.append(u_tail)

        u, q, k, v, lf = _inproj_rows(xs, gmpre, w_rows_s, bf_row, l, tm_s)
        u_tb = jnp.transpose(u.reshape(nbs, t_new, POOL_W), (1, 0, 2))
        pu = jnp.transpose(_pool_step(hist, u_tb, w_grp_b, scale3, l, past_len), (1, 0, 2))
        lf8 = lf[:, 0:N_HEADS].reshape(nbs, t_new, N_HEADS)
        ao = _decode(page_table, q.reshape(nbs, t_new, ATT_W), k.reshape(nbs, t_new, ATT_W),
                     v.reshape(nbs, t_new, ATT_W), lf8.reshape(nbs, 1, t_new * N_HEADS), ckt, cvt, cft, l)
        xs = _merge(xs, pu.reshape(-1, POOL_W), ao.reshape(-1, ATT_W), gmpre, w_gate_b, wpu_b, wau_b, wout_b,
                    gmpost, l, tm_s)
        ks_l.append(k.reshape(nbs, t_new, N_HEADS, HEAD_DIM)); vs_l.append(v.reshape(nbs, t_new, N_HEADS, HEAD_DIM))
        fs_l.append(lf8)
        ps_l.append(jnp.concatenate([state_pool[l], u.reshape(nbs, t_new, POOL_W)], axis=1)[:, t_new:])

        xp = _ffn(xp, g2pre, g2post, ffn2_wi_b, ffn2_wo_b, l, tm_p)
        xs = _ffn(xs, g2pre, g2post, ffn2_wi_b, ffn2_wo_b, l, tm_s)

    y_prompt = xp.reshape(nbp, seq_len, d)[:, N_META:]
    y_sample = xs.reshape(nbs, t_new, d)
    heads_t = lambda a: jnp.transpose(a.reshape(depth, nbp, N_HEADS, HEAD_DIM, seq_len), (0, 1, 4, 2, 3))
    return (y_prompt, y_sample,
            heads_t(k_all), heads_t(v_all), jnp.transpose(jnp.stack(fp_l), (0, 1, 3, 2)), jnp.stack(pp_l),
            jnp.stack(ks_l), jnp.stack(vs_l), jnp.stack(fs_l), jnp.stack(ps_l))
```
